```python
import math
import jax, jax.numpy as jnp
from jax import lax
import numpy as np

D_MODEL = 2048
BATCH = 2
SEQ = 16384
DEPTH = 2

N_MIXERS = 2
DIFF_HEADS = 8
DIFF_HEAD_DIM = D_MODEL // DIFF_HEADS // 2
FOX_HEADS = 16
FOX_HEAD_DIM = D_MODEL // FOX_HEADS
D_FF = 5632
REL_BUCKETS = 32
REL_MAX_DIST = 128
Q_BLOCK = 128
ALPHA = (2 * DEPTH) ** 0.25
BETA = (8 * DEPTH) ** -0.25
LN_EPS = 1e-5
N_DIFF = (DEPTH + 1) // 2
N_FOX = DEPTH // 2

kernel_name = "hybrid_diff_fox_macaron_deepnorm"


def layer_norm(x, g, b):
    xf = x.astype(jnp.float32)
    mu = jnp.mean(xf, axis=-1, keepdims=True)
    var = jnp.mean(jnp.square(xf - mu), axis=-1, keepdims=True)
    return ((xf - mu) * lax.rsqrt(var + LN_EPS) * g + b).astype(x.dtype)


def rms_norm(x, g):
    xf = x.astype(jnp.float32)
    return (xf * lax.rsqrt(jnp.mean(jnp.square(xf), axis=-1, keepdims=True) + LN_EPS) * g).astype(x.dtype)


def swiglu(x, wg, wu, wd):
    return (jax.nn.silu(x @ wg) * (x @ wu)) @ wd


def t5_bucket(rel):
    n = jnp.maximum(rel, 0)
    max_exact = REL_BUCKETS // 2
    nf = jnp.maximum(n, 1).astype(jnp.float32)
    large = max_exact + (jnp.log(nf / max_exact) / math.log(REL_MAX_DIST / max_exact)
                         * (REL_BUCKETS - max_exact)).astype(jnp.int32)
    large = jnp.minimum(large, REL_BUCKETS - 1)
    return jnp.where(n < max_exact, n, large)


def diff_attention(x, w_qkv, w_o, lq1, lk1, lq2, lk2, sub_g, rel_table, lam_init):
    B, S, D = x.shape
    H, d = DIFF_HEADS, DIFF_HEAD_DIM
    nb = S // Q_BLOCK
    q, k, v = jnp.split(x @ w_qkv, 3, axis=-1)
    q = q.reshape(B, S, H, 2, d)
    k = k.reshape(B, S, H, 2, d)
    v = v.reshape(B, S, H, 2 * d)
    scale = d ** -0.5
    lam = (jnp.exp(jnp.sum(lq1.astype(jnp.float32) * lk1.astype(jnp.float32)))
           - jnp.exp(jnp.sum(lq2.astype(jnp.float32) * lk2.astype(jnp.float32))) + lam_init)
    qb = q.reshape(B, nb, Q_BLOCK, H, 2, d).transpose(1, 0, 2, 3, 4, 5)
    kpos = jnp.arange(S)

    def block(args):
        qi, bi = args
        qpos = bi * Q_BLOCK + jnp.arange(Q_BLOCK)
        rel = qpos[:, None] - kpos[None, :]
        s = jnp.einsum('bqhmd,bkhmd->bhmqk', qi, k,
                       preferred_element_type=jnp.float32) * scale
        bias = rel_table.astype(jnp.float32)[t5_bucket(rel)].transpose(2, 0, 1)
        s = jnp.where(rel >= 0, s + bias[None, :, None], -jnp.inf)
        p = jax.nn.softmax(s, axis=-1)
        a = p[:, :, 0] - lam * p[:, :, 1]
        return jnp.einsum('bhqk,bkhe->bqhe', a.astype(v.dtype), v)

    o = lax.map(block, (qb, jnp.arange(nb)))
    o = o.transpose(1, 0, 2, 3, 4).reshape(B, S, H, 2 * d)
    o = rms_norm(o, sub_g) * (1.0 - lam_init)
    return o.reshape(B, S, D) @ w_o


def forgetting_attention(x, w_qkv, w_o, w_f, b_f):
    B, S, D = x.shape
    H, d = FOX_HEADS, FOX_HEAD_DIM
    nb = S // Q_BLOCK
    q, k, v = jnp.split(x @ w_qkv, 3, axis=-1)
    q = q.reshape(B, S, H, d)
    k = k.reshape(B, S, H, d)
    v = v.reshape(B, S, H, d)
    logf = jax.nn.log_sigmoid((x @ w_f + b_f).astype(jnp.float32))
    c = jnp.cumsum(logf, axis=1).transpose(0, 2, 1)
    qb = q.reshape(B, nb, Q_BLOCK, H, d).transpose(1, 0, 2, 3, 4)
    cb = c.reshape(B, H, nb, Q_BLOCK).transpose(2, 0, 1, 3)
    kpos = jnp.arange(S)
    scale = d ** -0.5

    def block(args):
        qi, ci, bi = args
        qpos = bi * Q_BLOCK + jnp.arange(Q_BLOCK)
        causal = qpos[:, None] >= kpos[None, :]
        s = jnp.einsum('bqhd,bkhd->bhqk', qi, k, preferred_element_type=jnp.float32) * scale
        s = s + ci[..., None] - c[:, :, None, :]
        s = jnp.where(causal, s, -jnp.inf)
        p = jax.nn.softmax(s, axis=-1)
        return jnp.einsum('bhqk,bkhd->bqhd', p.astype(v.dtype), v)

    o = lax.map(block, (qb, cb, jnp.arange(nb)))
    o = o.transpose(1, 0, 2, 3, 4).reshape(B, S, D)
    return o @ w_o


def setup_inputs(seed: int = 0) -> dict:
    key = jax.random.key(seed)
    ks = jax.random.split(key, 24)
    D, F = D_MODEL, D_FF
    sd = D ** -0.5
    nrm = lambda k, shp, s: jax.random.normal(k, shp, jnp.float32) * s

    def qkv_weight(k, n, out_qk):
        kq, kv = jax.random.split(k)
        qk = nrm(kq, (n, D, 2 * out_qk), sd)
        vv = nrm(kv, (n, D, out_qk), sd * BETA)
        return jnp.concatenate([qk, vv], axis=-1)

    return {
        "x": jax.random.normal(ks[0], (BATCH, SEQ, D), jnp.float32),
        "rel_table": nrm(ks[1], (REL_BUCKETS, DIFF_HEADS), 0.5),
        "ffn1_wg": nrm(ks[2], (DEPTH, D, F), sd),
        "ffn1_wu": nrm(ks[3], (DEPTH, D, F), sd * BETA),
        "ffn1_wd": nrm(ks[4], (DEPTH, F, D), F ** -0.5 * BETA),
        "ffn2_wg": nrm(ks[5], (DEPTH, D, F), sd),
        "ffn2_wu": nrm(ks[6], (DEPTH, D, F), sd * BETA),
        "ffn2_wd": nrm(ks[7], (DEPTH, F, D), F ** -0.5 * BETA),
        "ln_g": 1.0 + nrm(ks[8], (DEPTH, 3, D), 0.02),
        "ln_b": nrm(ks[9], (DEPTH, 3, D), 0.02),
        "diff_wqkv": qkv_weight(ks[10], N_DIFF, D),
        "diff_wo": nrm(ks[11], (N_DIFF, D, D), sd * BETA),
        "diff_lq1": nrm(ks[12], (N_DIFF, DIFF_HEAD_DIM), 0.1),
        "diff_lk1": nrm(ks[13], (N_DIFF, DIFF_HEAD_DIM), 0.1),
        "diff_lq2": nrm(ks[14], (N_DIFF, DIFF_HEAD_DIM), 0.1),
        "diff_lk2": nrm(ks[15], (N_DIFF, DIFF_HEAD_DIM), 0.1),
        "diff_subln_g": 1.0 + nrm(ks[16], (N_DIFF, 2 * DIFF_HEAD_DIM), 0.02),
        "fox_wqkv": qkv_weight(ks[17], N_FOX, D),
        "fox_wo": nrm(ks[18], (N_FOX, D, D), sd * BETA),
        "fox_wf": nrm(ks[19], (N_FOX, D, FOX_HEADS), sd),
        "fox_bf": jax.random.uniform(ks[20], (N_FOX, FOX_HEADS), jnp.float32, 1.0, 6.0),
    }


def reference(x, rel_table, ffn1_wg, ffn1_wu, ffn1_wd, ffn2_wg, ffn2_wu, ffn2_wd, ln_g, ln_b,
              diff_wqkv, diff_wo, diff_lq1, diff_lk1, diff_lq2, diff_lk2, diff_subln_g,
              fox_wqkv, fox_wo, fox_wf, fox_bf):
    for i in range(DEPTH):
        j = i // N_MIXERS
        x = layer_norm(ALPHA * x + 0.5 * swiglu(x, ffn1_wg[i], ffn1_wu[i], ffn1_wd[i]), ln_g[i, 0], ln_b[i, 0])
        if i % N_MIXERS == 0:
            lam_init = 0.8 - 0.6 * math.exp(-0.3 * i)
            m = diff_attention(x, diff_wqkv[j], diff_wo[j], diff_lq1[j], diff_lk1[j], diff_lq2[j],
                               diff_lk2[j], diff_subln_g[j], rel_table, lam_init)
        else:
            m = forgetting_attention(x, fox_wqkv[j], fox_wo[j], fox_wf[j], fox_bf[j])
        x = layer_norm(ALPHA * x + m, ln_g[i, 1], ln_b[i, 1])
        x = layer_norm(ALPHA * x + 0.5 * swiglu(x, ffn2_wg[i], ffn2_wu[i], ffn2_wd[i]), ln_g[i, 2], ln_b[i, 2])
    return x
```

```python
import functools
import math

import numpy as np
import jax
import jax.numpy as jnp
from jax import lax
from jax.experimental import pallas as pl
from jax.experimental.pallas import tpu as pltpu

HEAD_DIM = 128
REL_BUCKETS = 32
REL_MAX_DIST = 128
LN_EPS = 1e-5
LOG2E = 1.4426950408889634

LANES = 128
NEG_INIT = -1e30
NEG_MASK = -2e30

ATTN_TILE = 512
FFN_ROWS = 512
FFN_COLS = 512
PROJ_ROWS = 1024
PROJ_COLS = 1024
WO_ROWS = 512
GATE_ROWS = 512

VMEM_LIMIT = 56 * 1024 * 1024


def _bucket_upper_bounds():
    n = np.arange(0, 4 * REL_MAX_DIST)
    max_exact = REL_BUCKETS // 2
    nf = np.maximum(n, 1).astype(np.float32)
    large = max_exact + (np.log(nf / max_exact) / math.log(REL_MAX_DIST / max_exact)
                         * (REL_BUCKETS - max_exact)).astype(np.int32)
    bucket = np.where(n < max_exact, n, np.minimum(large, REL_BUCKETS - 1))
    assert np.all(np.diff(bucket) >= 0)
    return [int(n[bucket == b].max()) for b in range(REL_BUCKETS - 1)]


BUCKET_HI = _bucket_upper_bounds()
assert BUCKET_HI[-1] < LANES


def _layer_norm(y, g, b):
    mu = jnp.mean(y, axis=-1, keepdims=True)
    yc = y - mu
    var = jnp.mean(yc * yc, axis=-1, keepdims=True)
    return yc * lax.rsqrt(var + LN_EPS) * g + b


def _ffn_ln_kernel(x_ref, wgu_ref, wd_ref, g_ref, b_ref, o_ref, xb_ref, acc_ref, *, tf, alpha):
    j = pl.program_id(1)

    @pl.when(j == 0)
    def _():
        xb_ref[...] = x_ref[...].astype(jnp.bfloat16)
        acc_ref[...] = jnp.zeros_like(acc_ref)

    gu = jnp.dot(xb_ref[...], wgu_ref[0], preferred_element_type=jnp.float32)
    g = gu[:, :tf]
    u = gu[:, tf:]
    h = (g * (1.0 / (1.0 + jnp.exp(-g))) * u).astype(jnp.bfloat16)
    acc_ref[...] += jnp.dot(h, wd_ref[...], preferred_element_type=jnp.float32)

    @pl.when(j == pl.num_programs(1) - 1)
    def _():
        y = alpha * x_ref[...] + 0.5 * acc_ref[...]
        o_ref[...] = _layer_norm(y, g_ref[...], b_ref[...])


def _ffn_ln(x, wgu, wd, g, b, *, alpha):
    m, d = x.shape
    nf, _, tf2 = wgu.shape
    tf = tf2 // 2
    tm = FFN_ROWS
    return pl.pallas_call(
        functools.partial(_ffn_ln_kernel, tf=tf, alpha=alpha),
        grid=(m // tm, nf),
        in_specs=[
            pl.BlockSpec((tm, d), lambda i, j: (i, 0)),
            pl.BlockSpec((1, d, tf2), lambda i, j: (j, 0, 0)),
            pl.BlockSpec((tf, d), lambda i, j: (j, 0)),
            pl.BlockSpec((1, d), lambda i, j: (0, 0)),
            pl.BlockSpec((1, d), lambda i, j: (0, 0)),
        ],
        out_specs=pl.BlockSpec((tm, d), lambda i, j: (i, 0)),
        out_shape=jax.ShapeDtypeStruct((m, d), jnp.float32),
        scratch_shapes=[pltpu.VMEM((tm, d), jnp.bfloat16), pltpu.VMEM((tm, d), jnp.float32)],
        compiler_params=pltpu.CompilerParams(
            dimension_semantics=("arbitrary", "arbitrary"), vmem_limit_bytes=VMEM_LIMIT),
        name="ffn_ln",
    )(x, wgu, wd, g, b)


def _proj_kernel(x_ref, w_ref, *rest, q_blocks, q_scale, with_gate):
    if with_gate:
        wf_ref, o_ref, z_ref, xb_ref = rest
    else:
        o_ref, xb_ref = rest
    j = pl.program_id(1)

    @pl.when(j == 0)
    def _():
        xb_ref[...] = x_ref[...].astype(jnp.bfloat16)
        if with_gate:
            z_ref[...] = jnp.dot(xb_ref[...], wf_ref[...], preferred_element_type=jnp.float32)

    r = jnp.dot(xb_ref[...], w_ref[...], preferred_element_type=jnp.float32)
    r = r * jnp.where(j < q_blocks, q_scale, 1.0)
    o_ref[...] = r.astype(jnp.bfloat16)


def _qkv_proj(x, w, wf=None):
    m, d = x.shape
    n = w.shape[1]
    tm, tn = PROJ_ROWS, PROJ_COLS
    with_gate = wf is not None
    q_scale = HEAD_DIM ** -0.5 * LOG2E
    in_specs = [
        pl.BlockSpec((tm, d), lambda i, j: (i, 0)),
        pl.BlockSpec((d, tn), lambda i, j: (0, j)),
    ]
    out_specs = [pl.BlockSpec((tm, tn), lambda i, j: (i, j))]
    out_shape = [jax.ShapeDtypeStruct((m, n), jnp.bfloat16)]
    args = [x, w]
    if with_gate:
        in_specs.append(pl.BlockSpec((d, LANES), lambda i, j: (0, 0)))
        out_specs.append(pl.BlockSpec((tm, LANES), lambda i, j: (i, 0)))
        out_shape.append(jax.ShapeDtypeStruct((m, LANES), jnp.float32))
        args.append(wf)
    res = pl.pallas_call(
        functools.partial(_proj_kernel, q_blocks=(n // 3) // tn, q_scale=q_scale, with_gate=with_gate),
        grid=(m // tm, n // tn),
        in_specs=in_specs,
        out_specs=out_specs,
        out_shape=out_shape,
        scratch_shapes=[pltpu.VMEM((tm, d), jnp.bfloat16)],
        compiler_params=pltpu.CompilerParams(
            dimension_semantics=("arbitrary", "arbitrary"), vmem_limit_bytes=VMEM_LIMIT),
        name="qkv_proj_gate" if with_gate else "qkv_proj",
    )(*args)
    return res if with_gate else res[0]


def _split3(v):
    hi = v.astype(jnp.bfloat16).astype(jnp.float32)
    r1 = v - hi
    mid = r1.astype(jnp.bfloat16).astype(jnp.float32)
    lo = (r1 - mid).astype(jnp.bfloat16).astype(jnp.float32)
    return hi, mid, lo


def _gate_kernel(z_ref, bf_ref, aq_ref, ak_ref, carry_ref, *, heads):
    i = pl.program_id(1)
    ts = z_ref.shape[0]

    @pl.when(i == 0)
    def _():
        carry_ref[...] = jnp.zeros_like(carry_ref)

    z = z_ref[...] + bf_ref[...]
    logf = jnp.minimum(z, 0.0) - jnp.log1p(jnp.exp(-jnp.abs(z)))
    row = lax.broadcasted_iota(jnp.int32, (ts, ts), 0)
    col = lax.broadcasted_iota(jnp.int32, (ts, ts), 1)
    tri = jnp.where(row >= col, 1.0, 0.0).astype(jnp.float32)
    c = jnp.dot(tri, logf, preferred_element_type=jnp.float32,
                precision=lax.Precision.HIGHEST) + carry_ref[0:1, :]
    carry_ref[...] = jnp.broadcast_to(c[ts - 1:ts, :], carry_ref.shape)
    c = c * LOG2E
    lane = lax.broadcasted_iota(jnp.int32, (ts, LANES), 1)
    for h in range(heads):
        hi, mid, lo = _split3(jnp.broadcast_to(c[:, h:h + 1], (ts, LANES)))
        aq = jnp.where(lane == 0, hi, jnp.where(lane == 1, mid, jnp.where(lane == 2, lo,
             jnp.where(lane < 6, 1.0, 0.0))))
        ak = jnp.where(lane < 3, 1.0, jnp.where(lane == 3, -hi, jnp.where(lane == 4, -mid,
             jnp.where(lane == 5, -lo, 0.0))))
        aq_ref[:, h * LANES:(h + 1) * LANES] = aq.astype(jnp.bfloat16)
        ak_ref[:, h * LANES:(h + 1) * LANES] = ak.astype(jnp.bfloat16)


def _forget_gate(z, bf, *, batch, heads):
    m = z.shape[0]
    ts = GATE_ROWS
    ns = m // batch // ts
    spec = pl.BlockSpec((ts, heads * LANES), lambda b, i: (b * ns + i, 0))
    return pl.pallas_call(
        functools.partial(_gate_kernel, heads=heads),
        grid=(batch, ns),
        in_specs=[
            pl.BlockSpec((ts, LANES), lambda b, i: (b * ns + i, 0)),
            pl.BlockSpec((1, LANES), lambda b, i: (0, 0)),
        ],
        out_specs=[spec, spec],
        out_shape=[jax.ShapeDtypeStruct((m, heads * LANES), jnp.bfloat16)] * 2,
        scratch_shapes=[pltpu.VMEM((8, LANES), jnp.float32)],
        compiler_params=pltpu.CompilerParams(
            dimension_semantics=("arbitrary", "arbitrary"), vmem_limit_bytes=VMEM_LIMIT),
        name="forget_gate",
    )(z, bf)


def _softmax_tile(s_cols, v_tile, m_ref, l_ref, acc_ref):
    m_prev = m_ref[...]
    m_cur = s_cols[0]
    for sc in s_cols[1:]:
        m_cur = jnp.maximum(m_cur, sc)
    m_new = jnp.maximum(m_prev, jnp.max(m_cur, axis=1, keepdims=True))
    alpha = jnp.exp2(m_prev - m_new)
    p_cols = []
    l_add = None
    for sc in s_cols:
        pc = jnp.exp2(sc - m_new)
        l_add = pc if l_add is None else l_add + pc
        p_cols.append(pc.astype(jnp.bfloat16))
    p = jnp.concatenate(p_cols, axis=1)
    l_ref[...] = alpha * l_ref[...] + l_add
    pv = jnp.dot(p, v_tile, preferred_element_type=jnp.float32)
    dv = acc_ref.shape[-1]
    acc_ref[...] = acc_ref[...] * jnp.concatenate([alpha] * (dv // LANES), axis=1) + pv
    m_ref[...] = m_new


def _score_cols(q, k, col_bias=None):
    s = lax.dot_general(q, k, (((1,), (1,)), ((), ())), preferred_element_type=jnp.float32)
    cols = [s[:, c * LANES:(c + 1) * LANES] for c in range(s.shape[1] // LANES)]
    if col_bias is not None:
        cols = [sc if cb is None else sc + cb for sc, cb in zip(cols, col_bias)]
    return cols


def _stack_rows(blocks):
    return jnp.concatenate(blocks, axis=0)


def _diag_col_bias(t, diag_blk, sub_blk):
    nb = t // LANES
    masked = jnp.full((LANES, LANES), NEG_MASK, jnp.float32)
    zero = jnp.zeros((LANES, LANES), jnp.float32)
    out = []
    for cb in range(nb):
        blocks = [masked] * cb + [diag_blk]
        if cb + 1 < nb:
            blocks.append(sub_blk if sub_blk is not None else zero)
        blocks += [zero] * (nb - len(blocks))
        out.append(_stack_rows(blocks))
    return out


def _causal_block():
    r = lax.broadcasted_iota(jnp.int32, (LANES, LANES), 0)
    c = lax.broadcasted_iota(jnp.int32, (LANES, LANES), 1)
    return r - c


def _diff_attn_kernel(tab_ref, q_ref, k_ref, v_ref, lq1_ref, lk1_ref, lq2_ref, lk2_ref, subg_ref,
                      o_ref, m_ref, l_ref, acc_ref, bias_ref, *, t, lam_init):
    h = pl.program_id(1)
    qi = pl.program_id(2)
    d = HEAD_DIM
    nb = t // LANES

    @pl.when(qi == 0)
    def _():
        rel = _causal_block()
        last = tab_ref[REL_BUCKETS - 1, h]

        def table_of(dist):
            val = jnp.full((LANES, LANES), last, jnp.float32)
            for b in range(REL_BUCKETS - 2, -1, -1):
                val = jnp.where(dist <= BUCKET_HI[b], tab_ref[b, h], val)
            return (val - last) * LOG2E

        bias_ref[0] = jnp.where(rel >= 0, table_of(rel), NEG_MASK)
        bias_ref[1] = table_of(rel + LANES)

    m_ref[...] = jnp.full_like(m_ref, NEG_INIT)
    l_ref[...] = jnp.zeros_like(l_ref)
    acc_ref[...] = jnp.zeros_like(acc_ref)

    def tile(j, col_bias):
        start = pl.multiple_of(j * t, t)
        v_tile = v_ref[0, pl.ds(start, t), :]
        for mi in range(2):
            q = q_ref[0, :, mi * d:(mi + 1) * d]
            k = k_ref[0, pl.ds(start, t), mi * d:(mi + 1) * d]
            _softmax_tile(_score_cols(q, k, col_bias), v_tile, m_ref.at[mi], l_ref.at[mi], acc_ref.at[mi])

    def far_body(j, carry):
        tile(j, None)
        return carry

    lax.fori_loop(0, jnp.maximum(qi - 1, 0), far_body, 0)

    @pl.when(qi >= 1)
    def _():
        zero = jnp.zeros((LANES, LANES), jnp.float32)
        last_col = _stack_rows([bias_ref[1]] + [zero] * (nb - 1))
        tile(qi - 1, [None] * (nb - 1) + [last_col])

    tile(qi, _diag_col_bias(t, bias_ref[0], bias_ref[1]))

    lam = (jnp.exp(jnp.sum(lq1_ref[...] * lk1_ref[...], axis=1, keepdims=True))
           - jnp.exp(jnp.sum(lq2_ref[...] * lk2_ref[...], axis=1, keepdims=True)) + lam_init)
    o0 = acc_ref[0] * (1.0 / jnp.sum(l_ref[0], axis=1, keepdims=True))
    o1 = acc_ref[1] * (1.0 / jnp.sum(l_ref[1], axis=1, keepdims=True))
    o = o0 - lam * o1
    o = o * lax.rsqrt(jnp.mean(o * o, axis=-1, keepdims=True) + LN_EPS) * subg_ref[...] * (1.0 - lam_init)
    o_ref[0] = o.astype(o_ref.dtype)


def _diff_attention(qkv, rel_table, lq1, lk1, lq2, lk2, sub_g, *, lam_init):
    bsz, s, n3 = qkv.shape
    dm = n3 // 3
    hw = 2 * HEAD_DIM
    heads = dm // hw
    t = ATTN_TILE
    vec = pl.BlockSpec((1, HEAD_DIM), lambda b, h, i: (0, 0))
    return pl.pallas_call(
        functools.partial(_diff_attn_kernel, t=t, lam_init=lam_init),
        grid=(bsz, heads, s // t),
        in_specs=[
            pl.BlockSpec(memory_space=pltpu.SMEM),
            pl.BlockSpec((1, t, hw), lambda b, h, i: (b, i, h)),
            pl.BlockSpec((1, s, hw), lambda b, h, i: (b, 0, heads + h)),
            pl.BlockSpec((1, s, hw), lambda b, h, i: (b, 0, 2 * heads + h)),
            vec, vec, vec, vec,
            pl.BlockSpec((1, hw), lambda b, h, i: (0, 0)),
        ],
        out_specs=pl.BlockSpec((1, t, hw), lambda b, h, i: (b, i, h)),
        out_shape=jax.ShapeDtypeStruct((bsz, s, dm), jnp.bfloat16),
        scratch_shapes=[
            pltpu.VMEM((2, t, LANES), jnp.float32),
            pltpu.VMEM((2, t, LANES), jnp.float32),
            pltpu.VMEM((2, t, hw), jnp.float32),
            pltpu.VMEM((2, LANES, LANES), jnp.float32),
        ],
        compiler_params=pltpu.CompilerParams(
            dimension_semantics=("arbitrary", "arbitrary", "arbitrary"), vmem_limit_bytes=VMEM_LIMIT),
        name="diff_attn",
    )(rel_table, qkv, qkv, qkv, lq1, lk1, lq2, lk2, sub_g)


def _fox_attn_kernel(q_ref, aq_ref, k_ref, ak_ref, v_ref, o_ref, m_ref, l_ref, acc_ref, *, t):
    qi = pl.program_id(2)

    m_ref[...] = jnp.full_like(m_ref, NEG_INIT)
    l_ref[...] = jnp.zeros_like(l_ref)
    acc_ref[...] = jnp.zeros_like(acc_ref)

    def tile(j, col_bias):
        start = pl.multiple_of(j * t, t)
        q = jnp.concatenate([q_ref[0], aq_ref[0]], axis=1)
        k = jnp.concatenate([k_ref[0, pl.ds(start, t), :], ak_ref[0, pl.ds(start, t), :]], axis=1)
        v_tile = v_ref[0, pl.ds(start, t), :]
        _softmax_tile(_score_cols(q, k, col_bias), v_tile, m_ref, l_ref, acc_ref)

    def far_body(j, carry):
        tile(j, None)
        return carry

    lax.fori_loop(0, qi, far_body, 0)

    causal = jnp.where(_causal_block() >= 0, 0.0, NEG_MASK).astype(jnp.float32)
    tile(qi, _diag_col_bias(t, causal, None))

    o = acc_ref[...] * (1.0 / jnp.sum(l_ref[...], axis=1, keepdims=True))
    o_ref[0] = o.astype(o_ref.dtype)


def _fox_attention(qkv, aq, ak):
    bsz, s, n3 = qkv.shape
    dm = n3 // 3
    d = HEAD_DIM
    heads = dm // d
    t = ATTN_TILE
    q_spec = pl.BlockSpec((1, t, d), lambda b, h, i: (b, i, h))
    return pl.pallas_call(
        functools.partial(_fox_attn_kernel, t=t),
        grid=(bsz, heads, s // t),
        in_specs=[
            q_spec,
            q_spec,
            pl.BlockSpec((1, s, d), lambda b, h, i: (b, 0, heads + h)),
            pl.BlockSpec((1, s, d), lambda b, h, i: (b, 0, h)),
            pl.BlockSpec((1, s, d), lambda b, h, i: (b, 0, 2 * heads + h)),
        ],
        out_specs=q_spec,
        out_shape=jax.ShapeDtypeStruct((bsz, s, dm), jnp.bfloat16),
        scratch_shapes=[
            pltpu.VMEM((t, LANES), jnp.float32),
            pltpu.VMEM((t, LANES), jnp.float32),
            pltpu.VMEM((t, d), jnp.float32),
        ],
        compiler_params=pltpu.CompilerParams(
            dimension_semantics=("arbitrary", "arbitrary", "arbitrary"), vmem_limit_bytes=VMEM_LIMIT),
        name="fox_attn",
    )(qkv, aq, qkv, ak, qkv)


def _wo_ln_kernel(o_ref, w_ref, x_ref, g_ref, b_ref, out_ref, *, alpha):
    y = alpha * x_ref[...] + jnp.dot(o_ref[...], w_ref[...], preferred_element_type=jnp.float32)
    out_ref[...] = _layer_norm(y, g_ref[...], b_ref[...])


def _wo_ln(o, w, x, g, b, *, alpha):
    m, d = x.shape
    tm = WO_ROWS
    row = pl.BlockSpec((tm, d), lambda i: (i, 0))
    vec = pl.BlockSpec((1, d), lambda i: (0, 0))
    return pl.pallas_call(
        functools.partial(_wo_ln_kernel, alpha=alpha),
        grid=(m // tm,),
        in_specs=[row, pl.BlockSpec((d, d), lambda i: (0, 0)), row, vec, vec],
        out_specs=row,
        out_shape=jax.ShapeDtypeStruct((m, d), jnp.float32),
        compiler_params=pltpu.CompilerParams(
            dimension_semantics=("arbitrary",), vmem_limit_bytes=VMEM_LIMIT),
        name="wo_ln",
    )(o, w, x, g, b)


def _pack_gate_up(wg, wu):
    d, f = wg.shape
    tf = FFN_COLS
    wg = wg.astype(jnp.bfloat16).reshape(d, f // tf, tf)
    wu = wu.astype(jnp.bfloat16).reshape(d, f // tf, tf)
    return jnp.concatenate([wg, wu], axis=-1).transpose(1, 0, 2)


def kernel(x, rel_table, ffn1_wg, ffn1_wu, ffn1_wd, ffn2_wg, ffn2_wu, ffn2_wd, ln_g, ln_b, diff_wqkv, diff_wo, diff_lq1, diff_lk1, diff_lq2, diff_lk2, diff_subln_g, fox_wqkv, fox_wo, fox_wf, fox_bf):
    bsz, s, d = x.shape
    depth = ffn1_wg.shape[0]
    n_mixers = 2
    alpha = (2 * depth) ** 0.25
    bf16 = jnp.bfloat16
    xf = x.reshape(bsz * s, d)
    for i in range(depth):
        j = i // n_mixers
        xf = _ffn_ln(xf, _pack_gate_up(ffn1_wg[i], ffn1_wu[i]), ffn1_wd[i].astype(bf16),
                     ln_g[i, 0][None], ln_b[i, 0][None], alpha=alpha)
        if i % n_mixers == 0:
            lam_init = 0.8 - 0.6 * math.exp(-0.3 * i)
            qkv = _qkv_proj(xf, diff_wqkv[j].astype(bf16))
            o = _diff_attention(qkv.reshape(bsz, s, 3 * d), rel_table, diff_lq1[j][None], diff_lk1[j][None],
                                diff_lq2[j][None], diff_lk2[j][None], diff_subln_g[j][None],
                                lam_init=lam_init)
            wo = diff_wo[j]
        else:
            heads = fox_wf.shape[-1]
            wf = jnp.pad(fox_wf[j], ((0, 0), (0, LANES - heads))).astype(bf16)
            bf = jnp.pad(fox_bf[j], (0, LANES - heads))[None]
            qkv, z = _qkv_proj(xf, fox_wqkv[j].astype(bf16), wf)
            aq, ak = _forget_gate(z, bf, batch=bsz, heads=heads)
            o = _fox_attention(qkv.reshape(bsz, s, 3 * d), aq.reshape(bsz, s, d), ak.reshape(bsz, s, d))
            wo = fox_wo[j]
        xf = _wo_ln(o.reshape(bsz * s, d), wo.astype(bf16), xf, ln_g[i, 1][None], ln_b[i, 1][None],
                    alpha=alpha)
        xf = _ffn_ln(xf, _pack_gate_up(ffn2_wg[i], ffn2_wu[i]), ffn2_wd[i].astype(bf16),
                     ln_g[i, 2][None], ln_b[i, 2][None], alpha=alpha)
    return xf.reshape(bsz, s, d)
```

```python
import functools
import math

import numpy as np
import jax
import jax.numpy as jnp
from jax import lax
from jax.experimental import pallas as pl
from jax.experimental.pallas import tpu as pltpu

HEAD_DIM = 128
REL_BUCKETS = 32
REL_MAX_DIST = 128
LN_EPS = 1e-5
LOG2E = 1.4426950408889634

LANES = 128
SUBLANES = 8
NEG_INIT = -1e30
NEG_MASK = -2e30

ATTN_Q_TILE = 1024
ATTN_KV_TILE = 512
SOFTMAX_ROWS = 32
FFN_ROWS = 512
FFN_COLS = 512
PROJ_ROWS = 1024
PROJ_COLS = 1024
WO_ROWS = 512
GATE_ROWS = 512

VMEM_LIMIT = 56 * 1024 * 1024


def _bucket_upper_bounds():
    n = np.arange(0, 4 * REL_MAX_DIST)
    max_exact = REL_BUCKETS // 2
    nf = np.maximum(n, 1).astype(np.float32)
    large = max_exact + (np.log(nf / max_exact) / math.log(REL_MAX_DIST / max_exact)
                         * (REL_BUCKETS - max_exact)).astype(np.int32)
    bucket = np.where(n < max_exact, n, np.minimum(large, REL_BUCKETS - 1))
    assert np.all(np.diff(bucket) >= 0)
    return [int(n[bucket == b].max()) for b in range(REL_BUCKETS - 1)]


BUCKET_HI = _bucket_upper_bounds()
assert BUCKET_HI[-1] < LANES


def _layer_norm(y, g, b):
    mu = jnp.mean(y, axis=-1, keepdims=True)
    yc = y - mu
    var = jnp.mean(yc * yc, axis=-1, keepdims=True)
    return yc * lax.rsqrt(var + LN_EPS) * g + b


def _ffn_ln_kernel(x_ref, wgu_ref, wd_ref, g_ref, b_ref, o_ref, xb_ref, acc_ref, *, tf, alpha):
    j = pl.program_id(1)

    @pl.when(j == 0)
    def _():
        xb_ref[...] = x_ref[...].astype(jnp.bfloat16)
        acc_ref[...] = jnp.zeros_like(acc_ref)

    gu = jnp.dot(xb_ref[...], wgu_ref[0], preferred_element_type=jnp.float32)
    g = gu[:, :tf]
    u = gu[:, tf:]
    h = (g * (1.0 / (1.0 + jnp.exp(-g))) * u).astype(jnp.bfloat16)
    acc_ref[...] += jnp.dot(h, wd_ref[...], preferred_element_type=jnp.float32)

    @pl.when(j == pl.num_programs(1) - 1)
    def _():
        y = alpha * x_ref[...] + 0.5 * acc_ref[...]
        o_ref[...] = _layer_norm(y, g_ref[...], b_ref[...])


def _ffn_ln(x, wgu, wd, g, b, *, alpha):
    m, d = x.shape
    nf, _, tf2 = wgu.shape
    tf = tf2 // 2
    tm = FFN_ROWS
    return pl.pallas_call(
        functools.partial(_ffn_ln_kernel, tf=tf, alpha=alpha),
        grid=(m // tm, nf),
        in_specs=[
            pl.BlockSpec((tm, d), lambda i, j: (i, 0)),
            pl.BlockSpec((1, d, tf2), lambda i, j: (j, 0, 0)),
            pl.BlockSpec((tf, d), lambda i, j: (j, 0)),
            pl.BlockSpec((1, d), lambda i, j: (0, 0)),
            pl.BlockSpec((1, d), lambda i, j: (0, 0)),
        ],
        out_specs=pl.BlockSpec((tm, d), lambda i, j: (i, 0)),
        out_shape=jax.ShapeDtypeStruct((m, d), jnp.float32),
        scratch_shapes=[pltpu.VMEM((tm, d), jnp.bfloat16), pltpu.VMEM((tm, d), jnp.float32)],
        compiler_params=pltpu.CompilerParams(
            dimension_semantics=("arbitrary", "arbitrary"), vmem_limit_bytes=VMEM_LIMIT),
        name="ffn_ln",
    )(x, wgu, wd, g, b)


def _proj_kernel(x_ref, w_ref, *rest, q_blocks, qk_blocks, q_scale, tk, with_gate):
    if with_gate:
        wf_ref, qk_ref, vt_ref, z_ref, xb_ref = rest
    else:
        qk_ref, vt_ref, xb_ref = rest
    j = pl.program_id(1)

    @pl.when(j == 0)
    def _():
        xb_ref[...] = x_ref[...].astype(jnp.bfloat16)
        if with_gate:
            z_ref[...] = jnp.dot(xb_ref[...], wf_ref[...], preferred_element_type=jnp.float32)

    r = jnp.dot(xb_ref[...], w_ref[...], preferred_element_type=jnp.float32)

    @pl.when(j < qk_blocks)
    def _():
        qk_ref[...] = (r * jnp.where(j < q_blocks, q_scale, 1.0)).astype(jnp.bfloat16)

    @pl.when(j >= qk_blocks)
    def _():
        rt = r.T
        for t in range(vt_ref.shape[0]):
            vt_ref[t] = rt[:, t * tk:(t + 1) * tk].astype(jnp.bfloat16)


def _qkv_proj(x, w, wf=None):
    m, d = x.shape
    n = w.shape[1]
    dm = n // 3
    tm, tn, tk = PROJ_ROWS, PROJ_COLS, ATTN_KV_TILE
    with_gate = wf is not None
    q_scale = HEAD_DIM ** -0.5 * LOG2E
    qk_blocks = 2 * dm // tn
    in_specs = [
        pl.BlockSpec((tm, d), lambda i, j: (i, 0)),
        pl.BlockSpec((d, tn), lambda i, j: (0, j)),
    ]
    out_specs = [
        pl.BlockSpec((tm, tn), lambda i, j: (i, jnp.minimum(j, qk_blocks - 1))),
        pl.BlockSpec((tm // tk, tn, tk), lambda i, j: (i, jnp.maximum(j - qk_blocks, 0), 0)),
    ]
    out_shape = [
        jax.ShapeDtypeStruct((m, 2 * dm), jnp.bfloat16),
        jax.ShapeDtypeStruct((m // tk, dm, tk), jnp.bfloat16),
    ]
    args = [x, w]
    if with_gate:
        in_specs.append(pl.BlockSpec((d, LANES), lambda i, j: (0, 0)))
        out_specs.append(pl.BlockSpec((tm, LANES), lambda i, j: (i, 0)))
        out_shape.append(jax.ShapeDtypeStruct((m, LANES), jnp.float32))
        args.append(wf)
    return pl.pallas_call(
        functools.partial(_proj_kernel, q_blocks=dm // tn, qk_blocks=qk_blocks, q_scale=q_scale, tk=tk,
                          with_gate=with_gate),
        grid=(m // tm, n // tn),
        in_specs=in_specs,
        out_specs=out_specs,
        out_shape=out_shape,
        scratch_shapes=[pltpu.VMEM((tm, d), jnp.bfloat16)],
        compiler_params=pltpu.CompilerParams(
            dimension_semantics=("arbitrary", "arbitrary"), vmem_limit_bytes=VMEM_LIMIT),
        name="qkv_proj_gate" if with_gate else "qkv_proj",
    )(*args)


def _split3(v):
    hi = v.astype(jnp.bfloat16).astype(jnp.float32)
    r1 = v - hi
    mid = r1.astype(jnp.bfloat16).astype(jnp.float32)
    lo = (r1 - mid).astype(jnp.bfloat16).astype(jnp.float32)
    return hi, mid, lo


def _gate_kernel(z_ref, bf_ref, aq_ref, ak_ref, carry_ref, *, heads):
    i = pl.program_id(1)
    ts = z_ref.shape[0]

    @pl.when(i == 0)
    def _():
        carry_ref[...] = jnp.zeros_like(carry_ref)

    z = z_ref[...] + bf_ref[...]
    logf = jnp.minimum(z, 0.0) - jnp.log1p(jnp.exp(-jnp.abs(z)))
    row = lax.broadcasted_iota(jnp.int32, (ts, ts), 0)
    col = lax.broadcasted_iota(jnp.int32, (ts, ts), 1)
    tri = jnp.where(row >= col, 1.0, 0.0).astype(jnp.float32)
    c = jnp.dot(tri, logf, preferred_element_type=jnp.float32,
                precision=lax.Precision.HIGHEST) + carry_ref[0:1, :]
    carry_ref[...] = jnp.broadcast_to(c[ts - 1:ts, :], carry_ref.shape)
    c = c * LOG2E
    lane = lax.broadcasted_iota(jnp.int32, (ts, LANES), 1)
    for h in range(heads):
        hi, mid, lo = _split3(jnp.broadcast_to(c[:, h:h + 1], (ts, LANES)))
        aq = jnp.where(lane == 0, hi, jnp.where(lane == 1, mid, jnp.where(lane == 2, lo,
             jnp.where(lane < 6, 1.0, 0.0))))
        ak = jnp.where(lane < 3, 1.0, jnp.where(lane == 3, -hi, jnp.where(lane == 4, -mid,
             jnp.where(lane == 5, -lo, 0.0))))
        aq_ref[:, h * LANES:(h + 1) * LANES] = aq.astype(jnp.bfloat16)
        ak_ref[:, h * LANES:(h + 1) * LANES] = ak.astype(jnp.bfloat16)


def _forget_gate(z, bf, *, batch, heads):
    m = z.shape[0]
    ts = GATE_ROWS
    ns = m // batch // ts
    spec = pl.BlockSpec((ts, heads * LANES), lambda b, i: (b * ns + i, 0))
    return pl.pallas_call(
        functools.partial(_gate_kernel, heads=heads),
        grid=(batch, ns),
        in_specs=[
            pl.BlockSpec((ts, LANES), lambda b, i: (b * ns + i, 0)),
            pl.BlockSpec((1, LANES), lambda b, i: (0, 0)),
        ],
        out_specs=[spec, spec],
        out_shape=[jax.ShapeDtypeStruct((m, heads * LANES), jnp.bfloat16)] * 2,
        scratch_shapes=[pltpu.VMEM((8, LANES), jnp.float32)],
        compiler_params=pltpu.CompilerParams(
            dimension_semantics=("arbitrary", "arbitrary"), vmem_limit_bytes=VMEM_LIMIT),
        name="forget_gate",
    )(z, bf)


def _score_stage(k, q_t, s_ref, mt_ref, row_bias=None, corner=None):
    s = jnp.dot(k, q_t, preferred_element_type=jnp.float32)
    if row_bias is not None or corner is not None:
        nbk = s.shape[0] // LANES
        rows = [s[r * LANES:(r + 1) * LANES] for r in range(nbk)]
        if row_bias is not None:
            rows = [sr if rb is None else sr + rb for sr, rb in zip(rows, row_bias)]
        if corner is not None:
            rows[-1] = jnp.concatenate([rows[-1][:, :LANES] + corner, rows[-1][:, LANES:]], axis=1)
        s = jnp.concatenate(rows, axis=0)
    mt_ref[...] = jnp.max(s, axis=0, keepdims=True)
    s_ref[...] = s


def _exp_stage(s_ref, mt_ref, p_ref, a_ref, m_ref, l_ref):
    tk, _ = s_ref.shape
    m_prev = m_ref[...]
    m_new = jnp.maximum(m_prev, mt_ref[...])
    alpha = jnp.exp2(m_prev - m_new)
    m_ref[...] = m_new
    a_ref[...] = alpha
    l_add = None
    for r0 in range(0, tk, SOFTMAX_ROWS):
        p = jnp.exp2(s_ref[r0:r0 + SOFTMAX_ROWS, :] - m_new)
        for g in range(0, SOFTMAX_ROWS, SUBLANES):
            pg = p[g:g + SUBLANES]
            l_add = pg if l_add is None else l_add + pg
        p_ref[r0:r0 + SOFTMAX_ROWS, :] = p.astype(jnp.bfloat16)
    l_ref[...] = alpha * l_ref[...] + l_add


def _pv_stage(p_ref, a_ref, v_t, acc_ref):
    pv = jnp.dot(v_t, p_ref[...], preferred_element_type=jnp.float32)
    acc_ref[...] = acc_ref[...] * a_ref[...] + pv


def _near_row_bias(nbq, nbk, kv_off, diag_blk, sub_blk):
    masked = jnp.full((LANES, LANES), NEG_MASK, jnp.float32)
    zero = jnp.zeros((LANES, LANES), jnp.float32)
    out = []
    for kb in range(nbk):
        blocks = []
        for qb in range(nbq):
            delta = qb - (kv_off + kb)
            if delta < 0:
                blocks.append(masked)
            elif delta == 0:
                blocks.append(diag_blk)
            elif delta == 1 and sub_blk is not None:
                blocks.append(sub_blk)
            else:
                blocks.append(zero)
        out.append(jnp.concatenate(blocks, axis=1))
    return out


def _query_minus_kv():
    r = lax.broadcasted_iota(jnp.int32, (LANES, LANES), 0)
    c = lax.broadcasted_iota(jnp.int32, (LANES, LANES), 1)
    return c - r


def _attention_schedule(qi, scores, exps, pv, near_bias_a, near_bias_b):
    near_a, near_b = 2 * qi, 2 * qi + 1
    scores(near_a, 0, row_bias=near_bias_a)
    scores(near_b, 1, row_bias=near_bias_b)
    exps(0)

    @pl.when(qi == 0)
    def _():
        exps(1)
        pv(near_a, 0)
        pv(near_b, 1)

    @pl.when(qi > 0)
    def _():
        scores(0, 0)
        exps(1)
        pv(near_a, 0)
        scores(1, 1)
        exps(0)
        pv(near_b, 1)

        def far_pair(m, carry):
            exps(1)
            scores(2 * m, 0)
            pv(2 * m - 2, 0)
            exps(0)
            scores(2 * m + 1, 1)
            pv(2 * m - 1, 1)
            return carry

        lax.fori_loop(1, qi, far_pair, 0)
        exps(1)
        pv(2 * qi - 2, 0)
        pv(2 * qi - 1, 1)


def _diff_attn_kernel(tab_ref, q_ref, k_ref, vt_ref, lq1_ref, lk1_ref, lq2_ref, lk2_ref, subg_ref,
                      o_ref, qt_ref, s_ref, mt_ref, p_ref, a_ref, m_ref, l_ref, acc_ref, bias_ref,
                      *, tq, tk, lam_init):
    h = pl.program_id(1)
    qi = pl.program_id(2)
    d = HEAD_DIM
    nbq, nbk = tq // LANES, tk // LANES

    @pl.when(qi == 0)
    def _():
        rel = _query_minus_kv()
        last = tab_ref[REL_BUCKETS - 1, h]

        def table_of(dist):
            val = jnp.full((LANES, LANES), last, jnp.float32)
            for b in range(REL_BUCKETS - 2, -1, -1):
                val = jnp.where(dist <= BUCKET_HI[b], tab_ref[b, h], val)
            return (val - last) * LOG2E

        bias_ref[0] = jnp.where(rel >= 0, table_of(rel), NEG_MASK)
        bias_ref[1] = table_of(rel + LANES)

    m_ref[...] = jnp.full_like(m_ref, NEG_INIT)
    l_ref[...] = jnp.zeros_like(l_ref)
    acc_ref[...] = jnp.zeros_like(acc_ref)
    for mi in range(2):
        qt_ref[mi] = q_ref[0, :, mi * d:(mi + 1) * d].astype(jnp.float32).T.astype(jnp.bfloat16)

    diag_blk, sub_blk = bias_ref[0], bias_ref[1]

    def scores(j, slot, row_bias=None):
        start = pl.multiple_of(j * tk, tk)
        corner = None
        if row_bias is None and slot == 1:
            corner = sub_blk * jnp.where(j == 2 * qi - 1, 1.0, 0.0)
        for mi in range(2):
            _score_stage(k_ref[0, pl.ds(start, tk), mi * d:(mi + 1) * d], qt_ref[mi],
                         s_ref.at[slot, mi], mt_ref.at[slot, mi], row_bias, corner)

    def exps(slot):
        for mi in range(2):
            _exp_stage(s_ref.at[slot, mi], mt_ref.at[slot, mi], p_ref.at[slot, mi], a_ref.at[slot, mi],
                       m_ref.at[mi], l_ref.at[mi])

    def pv(j, slot):
        for mi in range(2):
            _pv_stage(p_ref.at[slot, mi], a_ref.at[slot, mi], vt_ref[j], acc_ref.at[mi])

    _attention_schedule(qi, scores, exps, pv,
                        _near_row_bias(nbq, nbk, 0, diag_blk, sub_blk),
                        _near_row_bias(nbq, nbk, nbk, diag_blk, sub_blk))

    lam = (jnp.exp(jnp.sum(lq1_ref[...] * lk1_ref[...], axis=1, keepdims=True))
           - jnp.exp(jnp.sum(lq2_ref[...] * lk2_ref[...], axis=1, keepdims=True)) + lam_init)
    o0 = (acc_ref[0] * (1.0 / jnp.sum(l_ref[0], axis=0, keepdims=True))).T
    o1 = (acc_ref[1] * (1.0 / jnp.sum(l_ref[1], axis=0, keepdims=True))).T
    o = o0 - lam * o1
    o = o * lax.rsqrt(jnp.mean(o * o, axis=-1, keepdims=True) + LN_EPS) * subg_ref[...] * (1.0 - lam_init)
    o_ref[0] = o.astype(o_ref.dtype)


def _diff_attention(qk, vt, rel_table, lq1, lk1, lq2, lk2, sub_g, *, batch, lam_init):
    m, dm2 = qk.shape
    dm = dm2 // 2
    s = m // batch
    hw = 2 * HEAD_DIM
    heads = dm // hw
    tq, tk = ATTN_Q_TILE, ATTN_KV_TILE
    assert tq == 2 * tk and s % tq == 0
    nq, nkv = s // tq, s // tk
    vec = pl.BlockSpec((1, HEAD_DIM), lambda b, h, i: (0, 0))
    q_spec = pl.BlockSpec((1, tq, hw), lambda b, h, i: (b, i, h))
    return pl.pallas_call(
        functools.partial(_diff_attn_kernel, tq=tq, tk=tk, lam_init=lam_init),
        grid=(batch, heads, nq),
        in_specs=[
            pl.BlockSpec(memory_space=pltpu.SMEM),
            q_spec,
            pl.BlockSpec((1, s, hw), lambda b, h, i: (b, 0, heads + h), pipeline_mode=pl.Buffered(1)),
            pl.BlockSpec((nkv, hw, tk), lambda b, h, i: (b, h, 0), pipeline_mode=pl.Buffered(1)),
            vec, vec, vec, vec,
            pl.BlockSpec((1, hw), lambda b, h, i: (0, 0)),
        ],
        out_specs=q_spec,
        out_shape=jax.ShapeDtypeStruct((batch, s, dm), jnp.bfloat16),
        scratch_shapes=[
            pltpu.VMEM((2, HEAD_DIM, tq), jnp.bfloat16),
            pltpu.VMEM((2, 2, tk, tq), jnp.float32),
            pltpu.VMEM((2, 2, 1, tq), jnp.float32),
            pltpu.VMEM((2, 2, tk, tq), jnp.bfloat16),
            pltpu.VMEM((2, 2, 1, tq), jnp.float32),
            pltpu.VMEM((2, 1, tq), jnp.float32),
            pltpu.VMEM((2, SUBLANES, tq), jnp.float32),
            pltpu.VMEM((2, hw, tq), jnp.float32),
            pltpu.VMEM((2, LANES, LANES), jnp.float32),
        ],
        compiler_params=pltpu.CompilerParams(
            dimension_semantics=("arbitrary", "arbitrary", "arbitrary"), vmem_limit_bytes=VMEM_LIMIT),
        name="diff_attn",
    )(rel_table, qk.reshape(batch, s, dm2), qk.reshape(batch, s, dm2), vt, lq1, lk1, lq2, lk2, sub_g)


def _fox_attn_kernel(q_ref, aq_ref, k_ref, ak_ref, vt_ref, o_ref, qt_ref, s_ref, mt_ref, p_ref, a_ref, m_ref,
                     l_ref, acc_ref, *, tq, tk):
    qi = pl.program_id(2)
    nbq, nbk = tq // LANES, tk // LANES

    m_ref[...] = jnp.full_like(m_ref, NEG_INIT)
    l_ref[...] = jnp.zeros_like(l_ref)
    acc_ref[...] = jnp.zeros_like(acc_ref)
    qa = jnp.concatenate([q_ref[0], aq_ref[0]], axis=1)
    qt_ref[...] = qa.astype(jnp.float32).T.astype(jnp.bfloat16)

    def scores(j, slot, row_bias=None):
        start = pl.multiple_of(j * tk, tk)
        k = jnp.concatenate([k_ref[0, pl.ds(start, tk), :], ak_ref[0, pl.ds(start, tk), :]], axis=1)
        _score_stage(k, qt_ref[...], s_ref.at[slot], mt_ref.at[slot], row_bias)

    def exps(slot):
        _exp_stage(s_ref.at[slot], mt_ref.at[slot], p_ref.at[slot], a_ref.at[slot], m_ref, l_ref)

    def pv(j, slot):
        _pv_stage(p_ref.at[slot], a_ref.at[slot], vt_ref[j], acc_ref)

    causal = jnp.where(_query_minus_kv() >= 0, 0.0, NEG_MASK).astype(jnp.float32)
    _attention_schedule(qi, scores, exps, pv,
                        _near_row_bias(nbq, nbk, 0, causal, None),
                        _near_row_bias(nbq, nbk, nbk, causal, None))

    o_t = acc_ref[...] * (1.0 / jnp.sum(l_ref[...], axis=0, keepdims=True))
    o_ref[0] = o_t.T.astype(o_ref.dtype)


def _fox_attention(qk, vt, aq, ak, *, batch):
    m, dm2 = qk.shape
    dm = dm2 // 2
    s = m // batch
    d = HEAD_DIM
    heads = dm // d
    tq, tk = ATTN_Q_TILE, ATTN_KV_TILE
    assert tq == 2 * tk and s % tq == 0
    nq, nkv = s // tq, s // tk
    q_spec = pl.BlockSpec((1, tq, d), lambda b, h, i: (b, i, h))

    def k_spec(col0):
        return pl.BlockSpec((1, s, d), lambda b, h, i: (b, 0, col0 + h), pipeline_mode=pl.Buffered(1))

    return pl.pallas_call(
        functools.partial(_fox_attn_kernel, tq=tq, tk=tk),
        grid=(batch, heads, nq),
        in_specs=[
            q_spec, q_spec, k_spec(heads), k_spec(0),
            pl.BlockSpec((nkv, d, tk), lambda b, h, i: (b, h, 0), pipeline_mode=pl.Buffered(1)),
        ],
        out_specs=q_spec,
        out_shape=jax.ShapeDtypeStruct((batch, s, dm), jnp.bfloat16),
        scratch_shapes=[
            pltpu.VMEM((2 * d, tq), jnp.bfloat16),
            pltpu.VMEM((2, tk, tq), jnp.float32),
            pltpu.VMEM((2, 1, tq), jnp.float32),
            pltpu.VMEM((2, tk, tq), jnp.bfloat16),
            pltpu.VMEM((2, 1, tq), jnp.float32),
            pltpu.VMEM((1, tq), jnp.float32),
            pltpu.VMEM((SUBLANES, tq), jnp.float32),
            pltpu.VMEM((d, tq), jnp.float32),
        ],
        compiler_params=pltpu.CompilerParams(
            dimension_semantics=("arbitrary", "arbitrary", "arbitrary"), vmem_limit_bytes=VMEM_LIMIT),
        name="fox_attn",
    )(qk.reshape(batch, s, dm2), aq.reshape(batch, s, dm), qk.reshape(batch, s, dm2),
      ak.reshape(batch, s, dm), vt)


def _wo_ln_kernel(o_ref, w_ref, x_ref, g_ref, b_ref, out_ref, *, alpha):
    y = alpha * x_ref[...] + jnp.dot(o_ref[...], w_ref[...], preferred_element_type=jnp.float32)
    out_ref[...] = _layer_norm(y, g_ref[...], b_ref[...])


def _wo_ln(o, w, x, g, b, *, alpha):
    m, d = x.shape
    tm = WO_ROWS
    row = pl.BlockSpec((tm, d), lambda i: (i, 0))
    vec = pl.BlockSpec((1, d), lambda i: (0, 0))
    return pl.pallas_call(
        functools.partial(_wo_ln_kernel, alpha=alpha),
        grid=(m // tm,),
        in_specs=[row, pl.BlockSpec((d, d), lambda i: (0, 0)), row, vec, vec],
        out_specs=row,
        out_shape=jax.ShapeDtypeStruct((m, d), jnp.float32),
        compiler_params=pltpu.CompilerParams(
            dimension_semantics=("arbitrary",), vmem_limit_bytes=VMEM_LIMIT),
        name="wo_ln",
    )(o, w, x, g, b)


def _pack_gate_up(wg, wu):
    d, f = wg.shape
    tf = FFN_COLS
    wg = wg.astype(jnp.bfloat16).reshape(d, f // tf, tf)
    wu = wu.astype(jnp.bfloat16).reshape(d, f // tf, tf)
    return jnp.concatenate([wg, wu], axis=-1).transpose(1, 0, 2)


def kernel(x, rel_table, ffn1_wg, ffn1_wu, ffn1_wd, ffn2_wg, ffn2_wu, ffn2_wd, ln_g, ln_b, diff_wqkv, diff_wo, diff_lq1, diff_lk1, diff_lq2, diff_lk2, diff_subln_g, fox_wqkv, fox_wo, fox_wf, fox_bf):
    bsz, s, d = x.shape
    depth = ffn1_wg.shape[0]
    n_mixers = 2
    alpha = (2 * depth) ** 0.25
    bf16 = jnp.bfloat16
    xf = x.reshape(bsz * s, d)
    for i in range(depth):
        j = i // n_mixers
        xf = _ffn_ln(xf, _pack_gate_up(ffn1_wg[i], ffn1_wu[i]), ffn1_wd[i].astype(bf16),
                     ln_g[i, 0][None], ln_b[i, 0][None], alpha=alpha)
        if i % n_mixers == 0:
            lam_init = 0.8 - 0.6 * math.exp(-0.3 * i)
            qk, vt = _qkv_proj(xf, diff_wqkv[j].astype(bf16))
            o = _diff_attention(qk, vt, rel_table, diff_lq1[j][None], diff_lk1[j][None],
                                diff_lq2[j][None], diff_lk2[j][None], diff_subln_g[j][None],
                                batch=bsz, lam_init=lam_init)
            wo = diff_wo[j]
        else:
            heads = fox_wf.shape[-1]
            wf = jnp.pad(fox_wf[j], ((0, 0), (0, LANES - heads))).astype(bf16)
            bf = jnp.pad(fox_bf[j], (0, LANES - heads))[None]
            qk, vt, z = _qkv_proj(xf, fox_wqkv[j].astype(bf16), wf)
            aq, ak = _forget_gate(z, bf, batch=bsz, heads=heads)
            o = _fox_attention(qk, vt, aq, ak, batch=bsz)
            wo = fox_wo[j]
        xf = _wo_ln(o.reshape(bsz * s, d), wo.astype(bf16), xf, ln_g[i, 1][None], ln_b[i, 1][None],
                    alpha=alpha)
        xf = _ffn_ln(xf, _pack_gate_up(ffn2_wg[i], ffn2_wu[i]), ffn2_wd[i].astype(bf16),
                     ln_g[i, 2][None], ln_b[i, 2][None], alpha=alpha)
    return xf.reshape(bsz, s, d)
```

```python
import functools
import math

import numpy as np
import jax
import jax.numpy as jnp
from jax import lax
from jax.experimental import pallas as pl
from jax.experimental.pallas import tpu as pltpu

HEAD_DIM = 128
REL_BUCKETS = 32
REL_MAX_DIST = 128
LN_EPS = 1e-5
LOG2E = 1.4426950408889634

LANES = 128
SUBLANES = 8
NEG_INIT = -1e30
NEG_MASK = -2e30

ATTN_Q_TILE = 1024
ATTN_KV_TILE = 512
SOFTMAX_ROWS = 32
FOX_HEADS_PER_STEP = 2
FFN_ROWS = 512
FFN_COLS = 512
PROJ_ROWS = 1024
PROJ_COLS = 1024
WO_ROWS = 512
GATE_ROWS = 512

VMEM_LIMIT = 56 * 1024 * 1024


def _bucket_upper_bounds():
    n = np.arange(0, 4 * REL_MAX_DIST)
    max_exact = REL_BUCKETS // 2
    nf = np.maximum(n, 1).astype(np.float32)
    large = max_exact + (np.log(nf / max_exact) / math.log(REL_MAX_DIST / max_exact)
                         * (REL_BUCKETS - max_exact)).astype(np.int32)
    bucket = np.where(n < max_exact, n, np.minimum(large, REL_BUCKETS - 1))
    assert np.all(np.diff(bucket) >= 0)
    return [int(n[bucket == b].max()) for b in range(REL_BUCKETS - 1)]


BUCKET_HI = _bucket_upper_bounds()
assert BUCKET_HI[-1] < LANES


def _layer_norm(y, g, b):
    mu = jnp.mean(y, axis=-1, keepdims=True)
    yc = y - mu
    var = jnp.mean(yc * yc, axis=-1, keepdims=True)
    return yc * lax.rsqrt(var + LN_EPS) * g + b


def _ffn_ln_kernel(x_ref, wg_ref, wu_ref, wd_ref, g_ref, b_ref, o_ref, xb_ref, h_ref, acc_ref, *, nf, alpha):
    j = pl.program_id(1)

    def up(slot):
        xb = xb_ref[...]
        g = jnp.dot(xb, wg_ref[...], preferred_element_type=jnp.float32)
        u = jnp.dot(xb, wu_ref[...], preferred_element_type=jnp.float32)
        h_ref[slot] = (g * (1.0 / (1.0 + jnp.exp(-g))) * u).astype(jnp.bfloat16)

    def down(slot):
        acc_ref[...] += jnp.dot(h_ref[slot], wd_ref[...], preferred_element_type=jnp.float32)

    @pl.when(j == 0)
    def _():
        xb_ref[...] = x_ref[...].astype(jnp.bfloat16)
        acc_ref[...] = jnp.zeros_like(acc_ref)
        up(0)

    for parity in (0, 1):
        @pl.when((j > 0) & (j < nf) & (j % 2 == parity))
        def _():
            up(parity)
            down(1 - parity)

    @pl.when(j == nf)
    def _():
        down((nf - 1) % 2)
        y = alpha * x_ref[...] + 0.5 * acc_ref[...]
        o_ref[...] = _layer_norm(y, g_ref[...], b_ref[...])


def _ffn_ln(x, wg, wu, wd, g, b, *, alpha):
    m, d = x.shape
    f = wg.shape[1]
    tm, tf = FFN_ROWS, FFN_COLS
    nf = f // tf
    return pl.pallas_call(
        functools.partial(_ffn_ln_kernel, nf=nf, alpha=alpha),
        grid=(m // tm, nf + 1),
        in_specs=[
            pl.BlockSpec((tm, d), lambda i, j: (i, 0)),
            pl.BlockSpec((d, tf), lambda i, j: (0, jnp.minimum(j, nf - 1))),
            pl.BlockSpec((d, tf), lambda i, j: (0, jnp.minimum(j, nf - 1))),
            pl.BlockSpec((tf, d), lambda i, j: (jnp.maximum(j - 1, 0), 0)),
            pl.BlockSpec((1, d), lambda i, j: (0, 0)),
            pl.BlockSpec((1, d), lambda i, j: (0, 0)),
        ],
        out_specs=pl.BlockSpec((tm, d), lambda i, j: (i, 0)),
        out_shape=jax.ShapeDtypeStruct((m, d), jnp.float32),
        scratch_shapes=[pltpu.VMEM((tm, d), jnp.bfloat16), pltpu.VMEM((2, tm, tf), jnp.bfloat16),
                        pltpu.VMEM((tm, d), jnp.float32)],
        compiler_params=pltpu.CompilerParams(
            dimension_semantics=("arbitrary", "arbitrary"), vmem_limit_bytes=VMEM_LIMIT),
        name="ffn_ln",
    )(x, wg, wu, wd, g, b)


def _proj_kernel(x_ref, w_ref, *rest, q_blocks, qk_blocks, q_scale, tk, with_gate):
    if with_gate:
        wf_ref, qk_ref, vt_ref, z_ref, xb_ref = rest
    else:
        qk_ref, vt_ref, xb_ref = rest
    j = pl.program_id(1)

    @pl.when(j == 0)
    def _():
        xb_ref[...] = x_ref[...].astype(jnp.bfloat16)
        if with_gate:
            z_ref[...] = jnp.dot(xb_ref[...], wf_ref[...], preferred_element_type=jnp.float32)

    r = jnp.dot(xb_ref[...], w_ref[...], preferred_element_type=jnp.float32)

    @pl.when(j < qk_blocks)
    def _():
        qk_ref[...] = (r * jnp.where(j < q_blocks, q_scale, 1.0)).astype(jnp.bfloat16)

    @pl.when(j >= qk_blocks)
    def _():
        rt = r.T
        for t in range(vt_ref.shape[0]):
            vt_ref[t] = rt[:, t * tk:(t + 1) * tk].astype(jnp.bfloat16)


def _qkv_proj(x, w, wf=None):
    m, d = x.shape
    n = w.shape[1]
    dm = n // 3
    tm, tn, tk = PROJ_ROWS, PROJ_COLS, ATTN_KV_TILE
    with_gate = wf is not None
    q_scale = HEAD_DIM ** -0.5 * LOG2E
    qk_blocks = 2 * dm // tn
    in_specs = [
        pl.BlockSpec((tm, d), lambda i, j: (i, 0)),
        pl.BlockSpec((d, tn), lambda i, j: (0, j)),
    ]
    out_specs = [
        pl.BlockSpec((tm, tn), lambda i, j: (i, jnp.minimum(j, qk_blocks - 1))),
        pl.BlockSpec((tm // tk, tn, tk), lambda i, j: (i, jnp.maximum(j - qk_blocks, 0), 0)),
    ]
    out_shape = [
        jax.ShapeDtypeStruct((m, 2 * dm), jnp.bfloat16),
        jax.ShapeDtypeStruct((m // tk, dm, tk), jnp.bfloat16),
    ]
    args = [x, w]
    if with_gate:
        in_specs.append(pl.BlockSpec((d, LANES), lambda i, j: (0, 0)))
        out_specs.append(pl.BlockSpec((tm, LANES), lambda i, j: (i, 0)))
        out_shape.append(jax.ShapeDtypeStruct((m, LANES), jnp.float32))
        args.append(wf)
    return pl.pallas_call(
        functools.partial(_proj_kernel, q_blocks=dm // tn, qk_blocks=qk_blocks, q_scale=q_scale, tk=tk,
                          with_gate=with_gate),
        grid=(m // tm, n // tn),
        in_specs=in_specs,
        out_specs=out_specs,
        out_shape=out_shape,
        scratch_shapes=[pltpu.VMEM((tm, d), jnp.bfloat16)],
        compiler_params=pltpu.CompilerParams(
            dimension_semantics=("arbitrary", "arbitrary"), vmem_limit_bytes=VMEM_LIMIT),
        name="qkv_proj_gate" if with_gate else "qkv_proj",
    )(*args)


def _split3(v):
    hi = v.astype(jnp.bfloat16).astype(jnp.float32)
    r1 = v - hi
    mid = r1.astype(jnp.bfloat16).astype(jnp.float32)
    lo = (r1 - mid).astype(jnp.bfloat16).astype(jnp.float32)
    return hi, mid, lo


def _gate_kernel(z_ref, bf_ref, aq_ref, ak_ref, carry_ref, *, heads):
    i = pl.program_id(1)
    ts = z_ref.shape[0]

    @pl.when(i == 0)
    def _():
        carry_ref[...] = jnp.zeros_like(carry_ref)

    z = z_ref[...] + bf_ref[...]
    logf = jnp.minimum(z, 0.0) - jnp.log1p(jnp.exp(-jnp.abs(z)))
    row = lax.broadcasted_iota(jnp.int32, (ts, ts), 0)
    col = lax.broadcasted_iota(jnp.int32, (ts, ts), 1)
    tri = jnp.where(row >= col, 1.0, 0.0).astype(jnp.float32)
    c = jnp.dot(tri, logf, preferred_element_type=jnp.float32,
                precision=lax.Precision.HIGHEST) + carry_ref[0:1, :]
    carry_ref[...] = jnp.broadcast_to(c[ts - 1:ts, :], carry_ref.shape)
    c = c * LOG2E
    lane = lax.broadcasted_iota(jnp.int32, (ts, LANES), 1)
    for h in range(heads):
        hi, mid, lo = _split3(jnp.broadcast_to(c[:, h:h + 1], (ts, LANES)))
        aq = jnp.where(lane == 0, hi, jnp.where(lane == 1, mid, jnp.where(lane == 2, lo,
             jnp.where(lane < 6, 1.0, 0.0))))
        ak = jnp.where(lane < 3, 1.0, jnp.where(lane == 3, -hi, jnp.where(lane == 4, -mid,
             jnp.where(lane == 5, -lo, 0.0))))
        aq_ref[:, h * LANES:(h + 1) * LANES] = aq.astype(jnp.bfloat16)
        ak_ref[:, h * LANES:(h + 1) * LANES] = ak.astype(jnp.bfloat16)


def _forget_gate(z, bf, *, batch, heads):
    m = z.shape[0]
    ts = GATE_ROWS
    ns = m // batch // ts
    spec = pl.BlockSpec((ts, heads * LANES), lambda b, i: (b * ns + i, 0))
    return pl.pallas_call(
        functools.partial(_gate_kernel, heads=heads),
        grid=(batch, ns),
        in_specs=[
            pl.BlockSpec((ts, LANES), lambda b, i: (b * ns + i, 0)),
            pl.BlockSpec((1, LANES), lambda b, i: (0, 0)),
        ],
        out_specs=[spec, spec],
        out_shape=[jax.ShapeDtypeStruct((m, heads * LANES), jnp.bfloat16)] * 2,
        scratch_shapes=[pltpu.VMEM((8, LANES), jnp.float32)],
        compiler_params=pltpu.CompilerParams(
            dimension_semantics=("arbitrary", "arbitrary"), vmem_limit_bytes=VMEM_LIMIT),
        name="forget_gate",
    )(z, bf)


def _score_stage(k, q_t, s_ref, mt_ref, row_bias=None, corner=None):
    s = jnp.dot(k, q_t, preferred_element_type=jnp.float32)
    if row_bias is not None or corner is not None:
        nbk = s.shape[0] // LANES
        rows = [s[r * LANES:(r + 1) * LANES] for r in range(nbk)]
        if row_bias is not None:
            rows = [sr if rb is None else sr + rb for sr, rb in zip(rows, row_bias)]
        if corner is not None:
            rows[-1] = jnp.concatenate([rows[-1][:, :LANES] + corner, rows[-1][:, LANES:]], axis=1)
        s = jnp.concatenate(rows, axis=0)
    mt_ref[...] = jnp.max(s, axis=0, keepdims=True)
    s_ref[...] = s


def _exp_stage(s_ref, mt_ref, p_ref, a_ref, m_ref, l_ref):
    tk, _ = s_ref.shape
    m_prev = m_ref[...]
    m_new = jnp.maximum(m_prev, mt_ref[...])
    alpha = jnp.exp2(m_prev - m_new)
    m_ref[...] = m_new
    a_ref[...] = alpha
    l_add = None
    for r0 in range(0, tk, SOFTMAX_ROWS):
        p = jnp.exp2(s_ref[r0:r0 + SOFTMAX_ROWS, :] - m_new)
        for g in range(0, SOFTMAX_ROWS, SUBLANES):
            pg = p[g:g + SUBLANES]
            l_add = pg if l_add is None else l_add + pg
        p_ref[r0:r0 + SOFTMAX_ROWS, :] = p.astype(jnp.bfloat16)
    l_ref[...] = alpha * l_ref[...] + l_add


def _pv_stage(p_ref, a_ref, v_t, acc_ref):
    pv = jnp.dot(v_t, p_ref[...], preferred_element_type=jnp.float32)
    acc_ref[...] = acc_ref[...] * a_ref[...] + pv


def _near_row_bias(nbq, nbk, kv_off, diag_blk, sub_blk):
    masked = jnp.full((LANES, LANES), NEG_MASK, jnp.float32)
    zero = jnp.zeros((LANES, LANES), jnp.float32)
    out = []
    for kb in range(nbk):
        blocks = []
        for qb in range(nbq):
            delta = qb - (kv_off + kb)
            if delta < 0:
                blocks.append(masked)
            elif delta == 0:
                blocks.append(diag_blk)
            elif delta == 1 and sub_blk is not None:
                blocks.append(sub_blk)
            else:
                blocks.append(zero)
        out.append(jnp.concatenate(blocks, axis=1))
    return out


def _query_minus_kv():
    r = lax.broadcasted_iota(jnp.int32, (LANES, LANES), 0)
    c = lax.broadcasted_iota(jnp.int32, (LANES, LANES), 1)
    return c - r


def _attention_schedule(qi, scores, exps, pv, near_bias_a, near_bias_b):
    near_a, near_b = 2 * qi, 2 * qi + 1
    scores(near_a, 0, row_bias=near_bias_a)
    scores(near_b, 1, row_bias=near_bias_b)
    exps(0)

    @pl.when(qi == 0)
    def _():
        exps(1)
        pv(near_a, 0)
        pv(near_b, 1)

    @pl.when(qi > 0)
    def _():
        scores(0, 0)
        exps(1)
        pv(near_a, 0)
        scores(1, 1)
        exps(0)
        pv(near_b, 1)

        def far_pair(m, carry):
            exps(1)
            scores(2 * m, 0)
            pv(2 * m - 2, 0)
            exps(0)
            scores(2 * m + 1, 1)
            pv(2 * m - 1, 1)
            return carry

        lax.fori_loop(1, qi, far_pair, 0)
        exps(1)
        pv(2 * qi - 2, 0)
        pv(2 * qi - 1, 1)


def _diff_attn_kernel(tab_ref, q_ref, k_ref, vt_ref, lq1_ref, lk1_ref, lq2_ref, lk2_ref, subg_ref,
                      o_ref, qt_ref, s_ref, mt_ref, p_ref, a_ref, m_ref, l_ref, acc_ref, bias_ref,
                      *, tq, tk, lam_init):
    h = pl.program_id(1)
    qi = pl.program_id(2)
    d = HEAD_DIM
    nbq, nbk = tq // LANES, tk // LANES

    @pl.when(qi == 0)
    def _():
        rel = _query_minus_kv()
        last = tab_ref[REL_BUCKETS - 1, h]

        def table_of(dist):
            val = jnp.full((LANES, LANES), last, jnp.float32)
            for b in range(REL_BUCKETS - 2, -1, -1):
                val = jnp.where(dist <= BUCKET_HI[b], tab_ref[b, h], val)
            return (val - last) * LOG2E

        bias_ref[0] = jnp.where(rel >= 0, table_of(rel), NEG_MASK)
        bias_ref[1] = table_of(rel + LANES)

    m_ref[...] = jnp.full_like(m_ref, NEG_INIT)
    l_ref[...] = jnp.zeros_like(l_ref)
    acc_ref[...] = jnp.zeros_like(acc_ref)
    for mi in range(2):
        qt_ref[mi] = q_ref[0, :, mi * d:(mi + 1) * d].astype(jnp.float32).T.astype(jnp.bfloat16)

    diag_blk, sub_blk = bias_ref[0], bias_ref[1]

    def scores(j, slot, row_bias=None):
        start = pl.multiple_of(j * tk, tk)
        corner = None
        if row_bias is None and slot == 1:
            corner = sub_blk * jnp.where(j == 2 * qi - 1, 1.0, 0.0)
        for mi in range(2):
            _score_stage(k_ref[0, pl.ds(start, tk), mi * d:(mi + 1) * d], qt_ref[mi],
                         s_ref.at[slot, mi], mt_ref.at[slot, mi], row_bias, corner)

    def exps(slot):
        for mi in range(2):
            _exp_stage(s_ref.at[slot, mi], mt_ref.at[slot, mi], p_ref.at[slot, mi], a_ref.at[slot, mi],
                       m_ref.at[mi], l_ref.at[mi])

    def pv(j, slot):
        for mi in range(2):
            _pv_stage(p_ref.at[slot, mi], a_ref.at[slot, mi], vt_ref[j], acc_ref.at[mi])

    _attention_schedule(qi, scores, exps, pv,
                        _near_row_bias(nbq, nbk, 0, diag_blk, sub_blk),
                        _near_row_bias(nbq, nbk, nbk, diag_blk, sub_blk))

    lam = (jnp.exp(jnp.sum(lq1_ref[...] * lk1_ref[...], axis=1, keepdims=True))
           - jnp.exp(jnp.sum(lq2_ref[...] * lk2_ref[...], axis=1, keepdims=True)) + lam_init)
    o0 = (acc_ref[0] * (1.0 / jnp.sum(l_ref[0], axis=0, keepdims=True))).T
    o1 = (acc_ref[1] * (1.0 / jnp.sum(l_ref[1], axis=0, keepdims=True))).T
    o = o0 - lam * o1
    o = o * lax.rsqrt(jnp.mean(o * o, axis=-1, keepdims=True) + LN_EPS) * subg_ref[...] * (1.0 - lam_init)
    o_ref[0] = o.astype(o_ref.dtype)


def _diff_attention(qk, vt, rel_table, lq1, lk1, lq2, lk2, sub_g, *, batch, lam_init):
    m, dm2 = qk.shape
    dm = dm2 // 2
    s = m // batch
    hw = 2 * HEAD_DIM
    heads = dm // hw
    tq, tk = ATTN_Q_TILE, ATTN_KV_TILE
    assert tq == 2 * tk and s % tq == 0
    nq, nkv = s // tq, s // tk
    vec = pl.BlockSpec((1, HEAD_DIM), lambda b, h, i: (0, 0))
    q_spec = pl.BlockSpec((1, tq, hw), lambda b, h, i: (b, i, h))
    return pl.pallas_call(
        functools.partial(_diff_attn_kernel, tq=tq, tk=tk, lam_init=lam_init),
        grid=(batch, heads, nq),
        in_specs=[
            pl.BlockSpec(memory_space=pltpu.SMEM),
            q_spec,
            pl.BlockSpec((1, s, hw), lambda b, h, i: (b, 0, heads + h), pipeline_mode=pl.Buffered(1)),
            pl.BlockSpec((nkv, hw, tk), lambda b, h, i: (b, h, 0), pipeline_mode=pl.Buffered(1)),
            vec, vec, vec, vec,
            pl.BlockSpec((1, hw), lambda b, h, i: (0, 0)),
        ],
        out_specs=q_spec,
        out_shape=jax.ShapeDtypeStruct((batch, s, dm), jnp.bfloat16),
        scratch_shapes=[
            pltpu.VMEM((2, HEAD_DIM, tq), jnp.bfloat16),
            pltpu.VMEM((2, 2, tk, tq), jnp.float32),
            pltpu.VMEM((2, 2, 1, tq), jnp.float32),
            pltpu.VMEM((2, 2, tk, tq), jnp.bfloat16),
            pltpu.VMEM((2, 2, 1, tq), jnp.float32),
            pltpu.VMEM((2, 1, tq), jnp.float32),
            pltpu.VMEM((2, SUBLANES, tq), jnp.float32),
            pltpu.VMEM((2, hw, tq), jnp.float32),
            pltpu.VMEM((2, LANES, LANES), jnp.float32),
        ],
        compiler_params=pltpu.CompilerParams(
            dimension_semantics=("arbitrary", "arbitrary", "arbitrary"), vmem_limit_bytes=VMEM_LIMIT),
        name="diff_attn",
    )(rel_table, qk.reshape(batch, s, dm2), qk.reshape(batch, s, dm2), vt, lq1, lk1, lq2, lk2, sub_g)


def _fox_attn_kernel(q_ref, aq_ref, k_ref, ak_ref, vt_ref, o_ref, qt_ref, s_ref, mt_ref, p_ref, a_ref, m_ref,
                     l_ref, acc_ref, *, tq, tk, hp):
    qi = pl.program_id(2)
    d = HEAD_DIM
    nbq, nbk = tq // LANES, tk // LANES

    m_ref[...] = jnp.full_like(m_ref, NEG_INIT)
    l_ref[...] = jnp.zeros_like(l_ref)
    acc_ref[...] = jnp.zeros_like(acc_ref)
    for hh in range(hp):
        cols = slice(hh * d, (hh + 1) * d)
        qa = jnp.concatenate([q_ref[0, :, cols], aq_ref[0, :, cols]], axis=1)
        qt_ref[hh] = qa.astype(jnp.float32).T.astype(jnp.bfloat16)

    def scores(j, slot, row_bias=None):
        start = pl.multiple_of(j * tk, tk)
        for hh in range(hp):
            cols = slice(hh * d, (hh + 1) * d)
            k = jnp.concatenate([k_ref[0, pl.ds(start, tk), cols], ak_ref[0, pl.ds(start, tk), cols]], axis=1)
            _score_stage(k, qt_ref[hh], s_ref.at[slot, hh], mt_ref.at[slot, hh], row_bias)

    def exps(slot):
        for hh in range(hp):
            _exp_stage(s_ref.at[slot, hh], mt_ref.at[slot, hh], p_ref.at[slot, hh], a_ref.at[slot, hh],
                       m_ref.at[hh], l_ref.at[hh])

    def pv(j, slot):
        for hh in range(hp):
            _pv_stage(p_ref.at[slot, hh], a_ref.at[slot, hh], vt_ref[j, hh * d:(hh + 1) * d, :], acc_ref.at[hh])

    causal = jnp.where(_query_minus_kv() >= 0, 0.0, NEG_MASK).astype(jnp.float32)
    _attention_schedule(qi, scores, exps, pv,
                        _near_row_bias(nbq, nbk, 0, causal, None),
                        _near_row_bias(nbq, nbk, nbk, causal, None))

    for hh in range(hp):
        o_t = acc_ref[hh] * (1.0 / jnp.sum(l_ref[hh], axis=0, keepdims=True))
        o_ref[0, :, hh * d:(hh + 1) * d] = o_t.T.astype(o_ref.dtype)


def _fox_attention(qk, vt, aq, ak, *, batch):
    m, dm2 = qk.shape
    dm = dm2 // 2
    s = m // batch
    d = HEAD_DIM
    hp = FOX_HEADS_PER_STEP
    groups = dm // (hp * d)
    tq, tk = ATTN_Q_TILE, ATTN_KV_TILE
    assert tq == 2 * tk and s % tq == 0
    nq, nkv = s // tq, s // tk
    q_spec = pl.BlockSpec((1, tq, hp * d), lambda b, g, i: (b, i, g))

    def k_spec(col0):
        return pl.BlockSpec((1, s, hp * d), lambda b, g, i: (b, 0, col0 + g), pipeline_mode=pl.Buffered(1))

    return pl.pallas_call(
        functools.partial(_fox_attn_kernel, tq=tq, tk=tk, hp=hp),
        grid=(batch, groups, nq),
        in_specs=[
            q_spec, q_spec, k_spec(groups), k_spec(0),
            pl.BlockSpec((nkv, hp * d, tk), lambda b, g, i: (b, g, 0), pipeline_mode=pl.Buffered(1)),
        ],
        out_specs=q_spec,
        out_shape=jax.ShapeDtypeStruct((batch, s, dm), jnp.bfloat16),
        scratch_shapes=[
            pltpu.VMEM((hp, 2 * d, tq), jnp.bfloat16),
            pltpu.VMEM((2, hp, tk, tq), jnp.float32),
            pltpu.VMEM((2, hp, 1, tq), jnp.float32),
            pltpu.VMEM((2, hp, tk, tq), jnp.bfloat16),
            pltpu.VMEM((2, hp, 1, tq), jnp.float32),
            pltpu.VMEM((hp, 1, tq), jnp.float32),
            pltpu.VMEM((hp, SUBLANES, tq), jnp.float32),
            pltpu.VMEM((hp, d, tq), jnp.float32),
        ],
        compiler_params=pltpu.CompilerParams(
            dimension_semantics=("arbitrary", "arbitrary", "arbitrary"), vmem_limit_bytes=VMEM_LIMIT),
        name="fox_attn",
    )(qk.reshape(batch, s, dm2), aq.reshape(batch, s, dm), qk.reshape(batch, s, dm2),
      ak.reshape(batch, s, dm), vt)


def _wo_ln_kernel(o_ref, w_ref, x_ref, g_ref, b_ref, out_ref, *, alpha):
    y = alpha * x_ref[...] + jnp.dot(o_ref[...], w_ref[...], preferred_element_type=jnp.float32)
    out_ref[...] = _layer_norm(y, g_ref[...], b_ref[...])


def _wo_ln(o, w, x, g, b, *, alpha):
    m, d = x.shape
    tm = WO_ROWS
    row = pl.BlockSpec((tm, d), lambda i: (i, 0))
    vec = pl.BlockSpec((1, d), lambda i: (0, 0))
    return pl.pallas_call(
        functools.partial(_wo_ln_kernel, alpha=alpha),
        grid=(m // tm,),
        in_specs=[row, pl.BlockSpec((d, d), lambda i: (0, 0)), row, vec, vec],
        out_specs=row,
        out_shape=jax.ShapeDtypeStruct((m, d), jnp.float32),
        compiler_params=pltpu.CompilerParams(
            dimension_semantics=("arbitrary",), vmem_limit_bytes=VMEM_LIMIT),
        name="wo_ln",
    )(o, w, x, g, b)


def kernel(x, rel_table, ffn1_wg, ffn1_wu, ffn1_wd, ffn2_wg, ffn2_wu, ffn2_wd, ln_g, ln_b, diff_wqkv, diff_wo, diff_lq1, diff_lk1, diff_lq2, diff_lk2, diff_subln_g, fox_wqkv, fox_wo, fox_wf, fox_bf):
    bsz, s, d = x.shape
    depth = ffn1_wg.shape[0]
    n_mixers = 2
    alpha = (2 * depth) ** 0.25
    bf16 = jnp.bfloat16
    xf = x.reshape(bsz * s, d)
    for i in range(depth):
        j = i // n_mixers
        xf = _ffn_ln(xf, ffn1_wg[i].astype(bf16), ffn1_wu[i].astype(bf16), ffn1_wd[i].astype(bf16),
                     ln_g[i, 0][None], ln_b[i, 0][None], alpha=alpha)
        if i % n_mixers == 0:
            lam_init = 0.8 - 0.6 * math.exp(-0.3 * i)
            qk, vt = _qkv_proj(xf, diff_wqkv[j].astype(bf16))
            o = _diff_attention(qk, vt, rel_table, diff_lq1[j][None], diff_lk1[j][None],
                                diff_lq2[j][None], diff_lk2[j][None], diff_subln_g[j][None],
                                batch=bsz, lam_init=lam_init)
            wo = diff_wo[j]
        else:
            heads = fox_wf.shape[-1]
            wf = jnp.pad(fox_wf[j], ((0, 0), (0, LANES - heads))).astype(bf16)
            bf = jnp.pad(fox_bf[j], (0, LANES - heads))[None]
            qk, vt, z = _qkv_proj(xf, fox_wqkv[j].astype(bf16), wf)
            aq, ak = _forget_gate(z, bf, batch=bsz, heads=heads)
            o = _fox_attention(qk, vt, aq, ak, batch=bsz)
            wo = fox_wo[j]
        xf = _wo_ln(o.reshape(bsz * s, d), wo.astype(bf16), xf, ln_g[i, 1][None], ln_b[i, 1][None],
                    alpha=alpha)
        xf = _ffn_ln(xf, ffn2_wg[i].astype(bf16), ffn2_wu[i].astype(bf16), ffn2_wd[i].astype(bf16),
                     ln_g[i, 2][None], ln_b[i, 2][None], alpha=alpha)
    return xf.reshape(bsz, s, d)
```

```python
import functools
import math

import numpy as np
import jax
import jax.numpy as jnp
from jax import lax
from jax.experimental import pallas as pl
from jax.experimental.pallas import tpu as pltpu

HEAD_DIM = 128
REL_BUCKETS = 32
REL_MAX_DIST = 128
LN_EPS = 1e-5
LOG2E = 1.4426950408889634

LANES = 128
SUBLANES = 8
NEG_INIT = -1e30
NEG_MASK = -2e30

ATTN_Q_TILE = 1024
ATTN_KV_TILE = 512
SOFTMAX_ROWS = 32
FOX_HEADS_PER_STEP = 2
FFN_ROWS = 512
FFN_COLS = 512
PROJ_ROWS = 1024
PROJ_COLS = 1024
WO_ROWS = 512
GATE_ROWS = 512

VMEM_LIMIT = 56 * 1024 * 1024


def _bucket_upper_bounds():
    n = np.arange(0, 4 * REL_MAX_DIST)
    max_exact = REL_BUCKETS // 2
    nf = np.maximum(n, 1).astype(np.float32)
    large = max_exact + (np.log(nf / max_exact) / math.log(REL_MAX_DIST / max_exact)
                         * (REL_BUCKETS - max_exact)).astype(np.int32)
    bucket = np.where(n < max_exact, n, np.minimum(large, REL_BUCKETS - 1))
    assert np.all(np.diff(bucket) >= 0)
    return [int(n[bucket == b].max()) for b in range(REL_BUCKETS - 1)]


BUCKET_HI = _bucket_upper_bounds()
assert BUCKET_HI[-1] < LANES


def _layer_norm(y, g, b):
    mu = jnp.mean(y, axis=-1, keepdims=True)
    yc = y - mu
    var = jnp.mean(yc * yc, axis=-1, keepdims=True)
    return yc * lax.rsqrt(var + LN_EPS) * g + b


def _ffn_ln_kernel(x_ref, *rest, nf, alpha, pipelined, interleaved):
    if interleaved:
        wgu_ref, wd_ref, g_ref, b_ref, o_ref, xb_ref, h_ref, acc_ref = rest
    else:
        wg_ref, wu_ref, wd_ref, g_ref, b_ref, o_ref, xb_ref, h_ref, acc_ref = rest
    j = pl.program_id(1)
    tf = wd_ref.shape[0]

    def up(slot):
        xb = xb_ref[...]
        if interleaved:
            gu = jnp.dot(xb, wgu_ref[...], preferred_element_type=jnp.float32)
            g, u = gu[:, :tf], gu[:, tf:]
        else:
            g = jnp.dot(xb, wg_ref[...], preferred_element_type=jnp.float32)
            u = jnp.dot(xb, wu_ref[...], preferred_element_type=jnp.float32)
        h_ref[slot] = (g * (1.0 / (1.0 + jnp.exp(-g))) * u).astype(jnp.bfloat16)

    def down(slot):
        acc_ref[...] += jnp.dot(h_ref[slot], wd_ref[...], preferred_element_type=jnp.float32)

    def finish():
        y = alpha * x_ref[...] + 0.5 * acc_ref[...]
        o_ref[...] = _layer_norm(y, g_ref[...], b_ref[...])

    if pipelined:
        @pl.when(j == 0)
        def _():
            xb_ref[...] = x_ref[...].astype(jnp.bfloat16)
            acc_ref[...] = jnp.zeros_like(acc_ref)
            up(0)

        for parity in (0, 1):
            @pl.when((j > 0) & (j < nf) & (j % 2 == parity))
            def _():
                up(parity)
                down(1 - parity)

        @pl.when(j == nf)
        def _():
            down((nf - 1) % 2)
            finish()
    else:
        @pl.when(j == 0)
        def _():
            xb_ref[...] = x_ref[...].astype(jnp.bfloat16)
            acc_ref[...] = jnp.zeros_like(acc_ref)

        up(0)
        down(0)

        @pl.when(j == nf - 1)
        def _():
            finish()


def _ffn_ln(x, wg, wu, wd, g, b, *, alpha, pipelined, interleaved):
    m, d = x.shape
    f = wg.shape[1]
    tm, tf = FFN_ROWS, FFN_COLS
    nf = f // tf
    if pipelined:
        steps = nf + 1
        up_idx = lambda j: jnp.minimum(j, nf - 1)
        down_idx = lambda j: jnp.maximum(j - 1, 0)
    else:
        steps = nf
        up_idx = lambda j: j
        down_idx = lambda j: j
    if interleaved:
        wgu = jnp.stack([wg.reshape(d, nf, tf), wu.reshape(d, nf, tf)], axis=2).reshape(d, 2 * f)
        w_args = [wgu]
        w_specs = [pl.BlockSpec((d, 2 * tf), lambda i, j: (0, up_idx(j)))]
    else:
        w_args = [wg, wu]
        w_specs = [pl.BlockSpec((d, tf), lambda i, j: (0, up_idx(j)))] * 2
    return pl.pallas_call(
        functools.partial(_ffn_ln_kernel, nf=nf, alpha=alpha, pipelined=pipelined, interleaved=interleaved),
        grid=(m // tm, steps),
        in_specs=[pl.BlockSpec((tm, d), lambda i, j: (i, 0))] + w_specs + [
            pl.BlockSpec((tf, d), lambda i, j: (down_idx(j), 0)),
            pl.BlockSpec((1, d), lambda i, j: (0, 0)),
            pl.BlockSpec((1, d), lambda i, j: (0, 0)),
        ],
        out_specs=pl.BlockSpec((tm, d), lambda i, j: (i, 0)),
        out_shape=jax.ShapeDtypeStruct((m, d), jnp.float32),
        scratch_shapes=[pltpu.VMEM((tm, d), jnp.bfloat16), pltpu.VMEM((2, tm, tf), jnp.bfloat16),
                        pltpu.VMEM((tm, d), jnp.float32)],
        compiler_params=pltpu.CompilerParams(
            dimension_semantics=("arbitrary", "arbitrary"), vmem_limit_bytes=VMEM_LIMIT),
        name="ffn_" + ("P" if pipelined else "S") + ("I" if interleaved else "U"),
    )(x, *w_args, wd, g, b)


def _proj_kernel(x_ref, w_ref, *rest, q_blocks, qk_blocks, q_scale, tk, with_gate):
    if with_gate:
        wf_ref, qk_ref, vt_ref, z_ref, xb_ref = rest
    else:
        qk_ref, vt_ref, xb_ref = rest
    j = pl.program_id(1)

    @pl.when(j == 0)
    def _():
        xb_ref[...] = x_ref[...].astype(jnp.bfloat16)
        if with_gate:
            z_ref[...] = jnp.dot(xb_ref[...], wf_ref[...], preferred_element_type=jnp.float32)

    r = jnp.dot(xb_ref[...], w_ref[...], preferred_element_type=jnp.float32)

    @pl.when(j < qk_blocks)
    def _():
        qk_ref[...] = (r * jnp.where(j < q_blocks, q_scale, 1.0)).astype(jnp.bfloat16)

    @pl.when(j >= qk_blocks)
    def _():
        rt = r.T
        for t in range(vt_ref.shape[0]):
            vt_ref[t] = rt[:, t * tk:(t + 1) * tk].astype(jnp.bfloat16)


def _qkv_proj(x, w, wf=None):
    m, d = x.shape
    n = w.shape[1]
    dm = n // 3
    tm, tn, tk = PROJ_ROWS, PROJ_COLS, ATTN_KV_TILE
    with_gate = wf is not None
    q_scale = HEAD_DIM ** -0.5 * LOG2E
    qk_blocks = 2 * dm // tn
    in_specs = [
        pl.BlockSpec((tm, d), lambda i, j: (i, 0)),
        pl.BlockSpec((d, tn), lambda i, j: (0, j)),
    ]
    out_specs = [
        pl.BlockSpec((tm, tn), lambda i, j: (i, jnp.minimum(j, qk_blocks - 1))),
        pl.BlockSpec((tm // tk, tn, tk), lambda i, j: (i, jnp.maximum(j - qk_blocks, 0), 0)),
    ]
    out_shape = [
        jax.ShapeDtypeStruct((m, 2 * dm), jnp.bfloat16),
        jax.ShapeDtypeStruct((m // tk, dm, tk), jnp.bfloat16),
    ]
    args = [x, w]
    if with_gate:
        in_specs.append(pl.BlockSpec((d, LANES), lambda i, j: (0, 0)))
        out_specs.append(pl.BlockSpec((tm, LANES), lambda i, j: (i, 0)))
        out_shape.append(jax.ShapeDtypeStruct((m, LANES), jnp.float32))
        args.append(wf)
    return pl.pallas_call(
        functools.partial(_proj_kernel, q_blocks=dm // tn, qk_blocks=qk_blocks, q_scale=q_scale, tk=tk,
                          with_gate=with_gate),
        grid=(m // tm, n // tn),
        in_specs=in_specs,
        out_specs=out_specs,
        out_shape=out_shape,
        scratch_shapes=[pltpu.VMEM((tm, d), jnp.bfloat16)],
        compiler_params=pltpu.CompilerParams(
            dimension_semantics=("arbitrary", "arbitrary"), vmem_limit_bytes=VMEM_LIMIT),
        name="qkv_proj_gate" if with_gate else "qkv_proj",
    )(*args)


def _split3(v):
    hi = v.astype(jnp.bfloat16).astype(jnp.float32)
    r1 = v - hi
    mid = r1.astype(jnp.bfloat16).astype(jnp.float32)
    lo = (r1 - mid).astype(jnp.bfloat16).astype(jnp.float32)
    return hi, mid, lo


def _gate_kernel(z_ref, bf_ref, aq_ref, ak_ref, carry_ref, *, heads):
    i = pl.program_id(1)
    ts = z_ref.shape[0]

    @pl.when(i == 0)
    def _():
        carry_ref[...] = jnp.zeros_like(carry_ref)

    z = z_ref[...] + bf_ref[...]
    logf = jnp.minimum(z, 0.0) - jnp.log1p(jnp.exp(-jnp.abs(z)))
    row = lax.broadcasted_iota(jnp.int32, (ts, ts), 0)
    col = lax.broadcasted_iota(jnp.int32, (ts, ts), 1)
    tri = jnp.where(row >= col, 1.0, 0.0).astype(jnp.float32)
    c = jnp.dot(tri, logf, preferred_element_type=jnp.float32,
                precision=lax.Precision.HIGHEST) + carry_ref[0:1, :]
    carry_ref[...] = jnp.broadcast_to(c[ts - 1:ts, :], carry_ref.shape)
    c = c * LOG2E
    lane = lax.broadcasted_iota(jnp.int32, (ts, LANES), 1)
    for h in range(heads):
        hi, mid, lo = _split3(jnp.broadcast_to(c[:, h:h + 1], (ts, LANES)))
        aq = jnp.where(lane == 0, hi, jnp.where(lane == 1, mid, jnp.where(lane == 2, lo,
             jnp.where(lane < 6, 1.0, 0.0))))
        ak = jnp.where(lane < 3, 1.0, jnp.where(lane == 3, -hi, jnp.where(lane == 4, -mid,
             jnp.where(lane == 5, -lo, 0.0))))
        aq_ref[:, h * LANES:(h + 1) * LANES] = aq.astype(jnp.bfloat16)
        ak_ref[:, h * LANES:(h + 1) * LANES] = ak.astype(jnp.bfloat16)


def _forget_gate(z, bf, *, batch, heads):
    m = z.shape[0]
    ts = GATE_ROWS
    ns = m // batch // ts
    spec = pl.BlockSpec((ts, heads * LANES), lambda b, i: (b * ns + i, 0))
    return pl.pallas_call(
        functools.partial(_gate_kernel, heads=heads),
        grid=(batch, ns),
        in_specs=[
            pl.BlockSpec((ts, LANES), lambda b, i: (b * ns + i, 0)),
            pl.BlockSpec((1, LANES), lambda b, i: (0, 0)),
        ],
        out_specs=[spec, spec],
        out_shape=[jax.ShapeDtypeStruct((m, heads * LANES), jnp.bfloat16)] * 2,
        scratch_shapes=[pltpu.VMEM((8, LANES), jnp.float32)],
        compiler_params=pltpu.CompilerParams(
            dimension_semantics=("arbitrary", "arbitrary"), vmem_limit_bytes=VMEM_LIMIT),
        name="forget_gate",
    )(z, bf)


def _score_stage(k, q_t, s_ref, mt_ref, row_bias=None, corner=None):
    s = jnp.dot(k, q_t, preferred_element_type=jnp.float32)
    if row_bias is not None or corner is not None:
        nbk = s.shape[0] // LANES
        rows = [s[r * LANES:(r + 1) * LANES] for r in range(nbk)]
        if row_bias is not None:
            rows = [sr if rb is None else sr + rb for sr, rb in zip(rows, row_bias)]
        if corner is not None:
            rows[-1] = jnp.concatenate([rows[-1][:, :LANES] + corner, rows[-1][:, LANES:]], axis=1)
        s = jnp.concatenate(rows, axis=0)
    mt_ref[...] = jnp.max(s, axis=0, keepdims=True)
    s_ref[...] = s


def _exp_stage(s_ref, mt_ref, p_ref, a_ref, m_ref, l_ref):
    tk, _ = s_ref.shape
    m_prev = m_ref[...]
    m_new = jnp.maximum(m_prev, mt_ref[...])
    alpha = jnp.exp2(m_prev - m_new)
    m_ref[...] = m_new
    a_ref[...] = alpha
    l_add = None
    for r0 in range(0, tk, SOFTMAX_ROWS):
        p = jnp.exp2(s_ref[r0:r0 + SOFTMAX_ROWS, :] - m_new)
        for g in range(0, SOFTMAX_ROWS, SUBLANES):
            pg = p[g:g + SUBLANES]
            l_add = pg if l_add is None else l_add + pg
        p_ref[r0:r0 + SOFTMAX_ROWS, :] = p.astype(jnp.bfloat16)
    l_ref[...] = alpha * l_ref[...] + l_add


def _pv_stage(p_ref, a_ref, v_t, acc_ref):
    pv = jnp.dot(v_t, p_ref[...], preferred_element_type=jnp.float32)
    acc_ref[...] = acc_ref[...] * a_ref[...] + pv


def _near_row_bias(nbq, nbk, kv_off, diag_blk, sub_blk):
    masked = jnp.full((LANES, LANES), NEG_MASK, jnp.float32)
    zero = jnp.zeros((LANES, LANES), jnp.float32)
    out = []
    for kb in range(nbk):
        blocks = []
        for qb in range(nbq):
            delta = qb - (kv_off + kb)
            if delta < 0:
                blocks.append(masked)
            elif delta == 0:
                blocks.append(diag_blk)
            elif delta == 1 and sub_blk is not None:
                blocks.append(sub_blk)
            else:
                blocks.append(zero)
        out.append(jnp.concatenate(blocks, axis=1))
    return out


def _query_minus_kv():
    r = lax.broadcasted_iota(jnp.int32, (LANES, LANES), 0)
    c = lax.broadcasted_iota(jnp.int32, (LANES, LANES), 1)
    return c - r


def _attention_schedule(qi, scores, exps, pv, near_bias_a, near_bias_b):
    near_a, near_b = 2 * qi, 2 * qi + 1
    scores(near_a, 0, row_bias=near_bias_a)
    scores(near_b, 1, row_bias=near_bias_b)
    exps(0)

    @pl.when(qi == 0)
    def _():
        exps(1)
        pv(near_a, 0)
        pv(near_b, 1)

    @pl.when(qi > 0)
    def _():
        scores(0, 0)
        exps(1)
        pv(near_a, 0)
        scores(1, 1)
        exps(0)
        pv(near_b, 1)

        def far_pair(m, carry):
            exps(1)
            scores(2 * m, 0)
            pv(2 * m - 2, 0)
            exps(0)
            scores(2 * m + 1, 1)
            pv(2 * m - 1, 1)
            return carry

        lax.fori_loop(1, qi, far_pair, 0)
        exps(1)
        pv(2 * qi - 2, 0)
        pv(2 * qi - 1, 1)


def _diff_attn_kernel(tab_ref, q_ref, k_ref, vt_ref, lq1_ref, lk1_ref, lq2_ref, lk2_ref, subg_ref,
                      o_ref, qt_ref, s_ref, mt_ref, p_ref, a_ref, m_ref, l_ref, acc_ref, bias_ref,
                      *, tq, tk, lam_init):
    h = pl.program_id(1)
    qi = pl.program_id(2)
    d = HEAD_DIM
    nbq, nbk = tq // LANES, tk // LANES

    @pl.when(qi == 0)
    def _():
        rel = _query_minus_kv()
        last = tab_ref[REL_BUCKETS - 1, h]

        def table_of(dist):
            val = jnp.full((LANES, LANES), last, jnp.float32)
            for b in range(REL_BUCKETS - 2, -1, -1):
                val = jnp.where(dist <= BUCKET_HI[b], tab_ref[b, h], val)
            return (val - last) * LOG2E

        bias_ref[0] = jnp.where(rel >= 0, table_of(rel), NEG_MASK)
        bias_ref[1] = table_of(rel + LANES)

    m_ref[...] = jnp.full_like(m_ref, NEG_INIT)
    l_ref[...] = jnp.zeros_like(l_ref)
    acc_ref[...] = jnp.zeros_like(acc_ref)
    for mi in range(2):
        qt_ref[mi] = q_ref[0, :, mi * d:(mi + 1) * d].astype(jnp.float32).T.astype(jnp.bfloat16)

    diag_blk, sub_blk = bias_ref[0], bias_ref[1]

    def scores(j, slot, row_bias=None):
        start = pl.multiple_of(j * tk, tk)
        corner = None
        if row_bias is None and slot == 1:
            corner = sub_blk * jnp.where(j == 2 * qi - 1, 1.0, 0.0)
        for mi in range(2):
            _score_stage(k_ref[0, pl.ds(start, tk), mi * d:(mi + 1) * d], qt_ref[mi],
                         s_ref.at[slot, mi], mt_ref.at[slot, mi], row_bias, corner)

    def exps(slot):
        for mi in range(2):
            _exp_stage(s_ref.at[slot, mi], mt_ref.at[slot, mi], p_ref.at[slot, mi], a_ref.at[slot, mi],
                       m_ref.at[mi], l_ref.at[mi])

    def pv(j, slot):
        for mi in range(2):
            _pv_stage(p_ref.at[slot, mi], a_ref.at[slot, mi], vt_ref[j], acc_ref.at[mi])

    _attention_schedule(qi, scores, exps, pv,
                        _near_row_bias(nbq, nbk, 0, diag_blk, sub_blk),
                        _near_row_bias(nbq, nbk, nbk, diag_blk, sub_blk))

    lam = (jnp.exp(jnp.sum(lq1_ref[...] * lk1_ref[...], axis=1, keepdims=True))
           - jnp.exp(jnp.sum(lq2_ref[...] * lk2_ref[...], axis=1, keepdims=True)) + lam_init)
    o0 = (acc_ref[0] * (1.0 / jnp.sum(l_ref[0], axis=0, keepdims=True))).T
    o1 = (acc_ref[1] * (1.0 / jnp.sum(l_ref[1], axis=0, keepdims=True))).T
    o = o0 - lam * o1
    o = o * lax.rsqrt(jnp.mean(o * o, axis=-1, keepdims=True) + LN_EPS) * subg_ref[...] * (1.0 - lam_init)
    o_ref[0] = o.astype(o_ref.dtype)


def _diff_attention(qk, vt, rel_table, lq1, lk1, lq2, lk2, sub_g, *, batch, lam_init):
    m, dm2 = qk.shape
    dm = dm2 // 2
    s = m // batch
    hw = 2 * HEAD_DIM
    heads = dm // hw
    tq, tk = ATTN_Q_TILE, ATTN_KV_TILE
    assert tq == 2 * tk and s % tq == 0
    nq, nkv = s // tq, s // tk
    vec = pl.BlockSpec((1, HEAD_DIM), lambda b, h, i: (0, 0))
    q_spec = pl.BlockSpec((1, tq, hw), lambda b, h, i: (b, i, h))
    return pl.pallas_call(
        functools.partial(_diff_attn_kernel, tq=tq, tk=tk, lam_init=lam_init),
        grid=(batch, heads, nq),
        in_specs=[
            pl.BlockSpec(memory_space=pltpu.SMEM),
            q_spec,
            pl.BlockSpec((1, s, hw), lambda b, h, i: (b, 0, heads + h), pipeline_mode=pl.Buffered(1)),
            pl.BlockSpec((nkv, hw, tk), lambda b, h, i: (b, h, 0), pipeline_mode=pl.Buffered(1)),
            vec, vec, vec, vec,
            pl.BlockSpec((1, hw), lambda b, h, i: (0, 0)),
        ],
        out_specs=q_spec,
        out_shape=jax.ShapeDtypeStruct((batch, s, dm), jnp.bfloat16),
        scratch_shapes=[
            pltpu.VMEM((2, HEAD_DIM, tq), jnp.bfloat16),
            pltpu.VMEM((2, 2, tk, tq), jnp.float32),
            pltpu.VMEM((2, 2, 1, tq), jnp.float32),
            pltpu.VMEM((2, 2, tk, tq), jnp.bfloat16),
            pltpu.VMEM((2, 2, 1, tq), jnp.float32),
            pltpu.VMEM((2, 1, tq), jnp.float32),
            pltpu.VMEM((2, SUBLANES, tq), jnp.float32),
            pltpu.VMEM((2, hw, tq), jnp.float32),
            pltpu.VMEM((2, LANES, LANES), jnp.float32),
        ],
        compiler_params=pltpu.CompilerParams(
            dimension_semantics=("arbitrary", "arbitrary", "arbitrary"), vmem_limit_bytes=VMEM_LIMIT),
        name="diff_attn",
    )(rel_table, qk.reshape(batch, s, dm2), qk.reshape(batch, s, dm2), vt, lq1, lk1, lq2, lk2, sub_g)


def _fox_attn_kernel(q_ref, aq_ref, k_ref, ak_ref, vt_ref, o_ref, qt_ref, s_ref, mt_ref, p_ref, a_ref, m_ref,
                     l_ref, acc_ref, *, tq, tk, hp):
    qi = pl.program_id(2)
    d = HEAD_DIM
    nbq, nbk = tq // LANES, tk // LANES

    m_ref[...] = jnp.full_like(m_ref, NEG_INIT)
    l_ref[...] = jnp.zeros_like(l_ref)
    acc_ref[...] = jnp.zeros_like(acc_ref)
    for hh in range(hp):
        cols = slice(hh * d, (hh + 1) * d)
        qa = jnp.concatenate([q_ref[0, :, cols], aq_ref[0, :, cols]], axis=1)
        qt_ref[hh] = qa.astype(jnp.float32).T.astype(jnp.bfloat16)

    def scores(j, slot, row_bias=None):
        start = pl.multiple_of(j * tk, tk)
        for hh in range(hp):
            cols = slice(hh * d, (hh + 1) * d)
            k = jnp.concatenate([k_ref[0, pl.ds(start, tk), cols], ak_ref[0, pl.ds(start, tk), cols]], axis=1)
            _score_stage(k, qt_ref[hh], s_ref.at[slot, hh], mt_ref.at[slot, hh], row_bias)

    def exps(slot):
        for hh in range(hp):
            _exp_stage(s_ref.at[slot, hh], mt_ref.at[slot, hh], p_ref.at[slot, hh], a_ref.at[slot, hh],
                       m_ref.at[hh], l_ref.at[hh])

    def pv(j, slot):
        for hh in range(hp):
            _pv_stage(p_ref.at[slot, hh], a_ref.at[slot, hh], vt_ref[j, hh * d:(hh + 1) * d, :], acc_ref.at[hh])

    causal = jnp.where(_query_minus_kv() >= 0, 0.0, NEG_MASK).astype(jnp.float32)
    _attention_schedule(qi, scores, exps, pv,
                        _near_row_bias(nbq, nbk, 0, causal, None),
                        _near_row_bias(nbq, nbk, nbk, causal, None))

    for hh in range(hp):
        o_t = acc_ref[hh] * (1.0 / jnp.sum(l_ref[hh], axis=0, keepdims=True))
        o_ref[0, :, hh * d:(hh + 1) * d] = o_t.T.astype(o_ref.dtype)


def _fox_attention(qk, vt, aq, ak, *, batch):
    m, dm2 = qk.shape
    dm = dm2 // 2
    s = m // batch
    d = HEAD_DIM
    hp = FOX_HEADS_PER_STEP
    groups = dm // (hp * d)
    tq, tk = ATTN_Q_TILE, ATTN_KV_TILE
    assert tq == 2 * tk and s % tq == 0
    nq, nkv = s // tq, s // tk
    q_spec = pl.BlockSpec((1, tq, hp * d), lambda b, g, i: (b, i, g))

    def k_spec(col0):
        return pl.BlockSpec((1, s, hp * d), lambda b, g, i: (b, 0, col0 + g), pipeline_mode=pl.Buffered(1))

    return pl.pallas_call(
        functools.partial(_fox_attn_kernel, tq=tq, tk=tk, hp=hp),
        grid=(batch, groups, nq),
        in_specs=[
            q_spec, q_spec, k_spec(groups), k_spec(0),
            pl.BlockSpec((nkv, hp * d, tk), lambda b, g, i: (b, g, 0), pipeline_mode=pl.Buffered(1)),
        ],
        out_specs=q_spec,
        out_shape=jax.ShapeDtypeStruct((batch, s, dm), jnp.bfloat16),
        scratch_shapes=[
            pltpu.VMEM((hp, 2 * d, tq), jnp.bfloat16),
            pltpu.VMEM((2, hp, tk, tq), jnp.float32),
            pltpu.VMEM((2, hp, 1, tq), jnp.float32),
            pltpu.VMEM((2, hp, tk, tq), jnp.bfloat16),
            pltpu.VMEM((2, hp, 1, tq), jnp.float32),
            pltpu.VMEM((hp, 1, tq), jnp.float32),
            pltpu.VMEM((hp, SUBLANES, tq), jnp.float32),
            pltpu.VMEM((hp, d, tq), jnp.float32),
        ],
        compiler_params=pltpu.CompilerParams(
            dimension_semantics=("arbitrary", "arbitrary", "arbitrary"), vmem_limit_bytes=VMEM_LIMIT),
        name="fox_attn",
    )(qk.reshape(batch, s, dm2), aq.reshape(batch, s, dm), qk.reshape(batch, s, dm2),
      ak.reshape(batch, s, dm), vt)


def _wo_ln_kernel(o_ref, w_ref, x_ref, g_ref, b_ref, out_ref, *, alpha):
    y = alpha * x_ref[...] + jnp.dot(o_ref[...], w_ref[...], preferred_element_type=jnp.float32)
    out_ref[...] = _layer_norm(y, g_ref[...], b_ref[...])


def _wo_ln(o, w, x, g, b, *, alpha):
    m, d = x.shape
    tm = WO_ROWS
    row = pl.BlockSpec((tm, d), lambda i: (i, 0))
    vec = pl.BlockSpec((1, d), lambda i: (0, 0))
    return pl.pallas_call(
        functools.partial(_wo_ln_kernel, alpha=alpha),
        grid=(m // tm,),
        in_specs=[row, pl.BlockSpec((d, d), lambda i: (0, 0)), row, vec, vec],
        out_specs=row,
        out_shape=jax.ShapeDtypeStruct((m, d), jnp.float32),
        compiler_params=pltpu.CompilerParams(
            dimension_semantics=("arbitrary",), vmem_limit_bytes=VMEM_LIMIT),
        name="wo_ln",
    )(o, w, x, g, b)


def kernel(x, rel_table, ffn1_wg, ffn1_wu, ffn1_wd, ffn2_wg, ffn2_wu, ffn2_wd, ln_g, ln_b, diff_wqkv, diff_wo, diff_lq1, diff_lk1, diff_lq2, diff_lk2, diff_subln_g, fox_wqkv, fox_wo, fox_wf, fox_bf):
    bsz, s, d = x.shape
    depth = ffn1_wg.shape[0]
    n_mixers = 2
    alpha = (2 * depth) ** 0.25
    bf16 = jnp.bfloat16
    xf = x.reshape(bsz * s, d)
    for i in range(depth):
        j = i // n_mixers
        xf = _ffn_ln(xf, ffn1_wg[i].astype(bf16), ffn1_wu[i].astype(bf16), ffn1_wd[i].astype(bf16),
                     ln_g[i, 0][None], ln_b[i, 0][None], alpha=alpha, pipelined=True, interleaved=(i == 1))
        if i % n_mixers == 0:
            lam_init = 0.8 - 0.6 * math.exp(-0.3 * i)
            qk, vt = _qkv_proj(xf, diff_wqkv[j].astype(bf16))
            o = _diff_attention(qk, vt, rel_table, diff_lq1[j][None], diff_lk1[j][None],
                                diff_lq2[j][None], diff_lk2[j][None], diff_subln_g[j][None],
                                batch=bsz, lam_init=lam_init)
            wo = diff_wo[j]
        else:
            heads = fox_wf.shape[-1]
            wf = jnp.pad(fox_wf[j], ((0, 0), (0, LANES - heads))).astype(bf16)
            bf = jnp.pad(fox_bf[j], (0, LANES - heads))[None]
            qk, vt, z = _qkv_proj(xf, fox_wqkv[j].astype(bf16), wf)
            aq, ak = _forget_gate(z, bf, batch=bsz, heads=heads)
            o = _fox_attention(qk, vt, aq, ak, batch=bsz)
            wo = fox_wo[j]
        xf = _wo_ln(o.reshape(bsz * s, d), wo.astype(bf16), xf, ln_g[i, 1][None], ln_b[i, 1][None],
                    alpha=alpha)
        xf = _ffn_ln(xf, ffn2_wg[i].astype(bf16), ffn2_wu[i].astype(bf16), ffn2_wd[i].astype(bf16),
                     ln_g[i, 2][None], ln_b[i, 2][None], alpha=alpha, pipelined=False, interleaved=(i == 1))
    return xf.reshape(bsz, s, d)
```
